```python
import math
import jax
import jax.numpy as jnp
from jax import lax
import numpy as np

D_MODEL = 1024
BATCH = 32
SEQ = 2048
DEPTH = 1

CTX_LEN = 256
GRID_W = 64
SSD_D_INNER = 2 * D_MODEL
SSD_HEADDIM = 64
SSD_HEADS = SSD_D_INNER // SSD_HEADDIM
SSD_GROUPS = 8
SSD_HPG = SSD_HEADS // SSD_GROUPS
SSD_STATE = 128
SSD_CHUNK = 128
CONV_K = 4
CONV_LEFT = 2
LRU_WIDTH = D_MODEL
LRU_BLOCKS = 8
LRU_BLOCK_W = LRU_WIDTH // LRU_BLOCKS
LRU_C = 8.0
MLP_HIDDEN = 4 * D_MODEL
N_BRANCH = 2
N_MOD = 6
DEEPNORM_ALPHA = (2 * DEPTH) ** 0.25
DEEPNORM_BETA = (8 * DEPTH) ** -0.25
LN_EPS = 1e-6
RMS_EPS = 1e-5

SSD_BC_W = SSD_GROUPS * SSD_STATE
SSD_XB = SSD_D_INNER + SSD_BC_W
SSD_XBC = SSD_D_INNER + 2 * SSD_BC_W
SSD_DT = 2 * SSD_HEADS
O_DT = SSD_XB
O_LRU = O_DT + SSD_DT
STATE_COLS = O_LRU + LRU_WIDTH
O_C = STATE_COLS
O_Z = O_C + SSD_BC_W
O_LRU_GATE = O_Z + SSD_D_INNER
O_MERGE = O_LRU_GATE + LRU_WIDTH
IN_COLS = O_MERGE + N_BRANCH * D_MODEL

kernel_name = 'hybrid_ssd_rglru_dit_block'


def layer_norm(x, g=None, b=None):
    xf = x.astype(jnp.float32)
    mu = jnp.mean(xf, axis=-1, keepdims=True)
    var = jnp.mean(jnp.square(xf - mu), axis=-1, keepdims=True)
    y = (xf - mu) * lax.rsqrt(var + LN_EPS)
    if g is not None:
        y = y * g.astype(jnp.float32) + b.astype(jnp.float32)
    return y.astype(x.dtype)


def modulation(cvec, w_mod, b_mod, n_chunks):
    m = jax.nn.silu(cvec) @ w_mod[:, :n_chunks * D_MODEL] + b_mod[:n_chunks * D_MODEL]
    return jnp.split(m, n_chunks, axis=-1)


def modulate(x, shift, scale):
    return layer_norm(x) * (1.0 + scale) + shift


def short_conv(u, w, b, rows):
    bsz, t, ch = u.shape
    v = u if rows is None else u.reshape(bsz, rows, GRID_W, ch)
    n = v.shape[-2]
    pad = [(0, 0)] * (v.ndim - 2) + [(CONV_LEFT, CONV_K - 1 - CONV_LEFT), (0, 0)]
    vp = jnp.pad(v, pad)
    out = b
    for k in range(CONV_K):
        out = out + vp[..., k:k + n, :] * w[k]
    return out.reshape(bsz, t, ch)


def ssd_chunked(xh, dt, a_neg, bm, cm, h0):
    bsz, t = xh.shape[:2]
    nc = t // SSD_CHUNK
    xc = (xh.astype(jnp.float32) * dt[..., None]).reshape(bsz, nc, SSD_CHUNK, SSD_GROUPS, SSD_HPG, SSD_HEADDIM)
    cum = jnp.cumsum((dt * a_neg).reshape(bsz, nc, SSD_CHUNK, SSD_GROUPS, SSD_HPG), axis=2)
    bc = bm.astype(jnp.float32).reshape(bsz, nc, SSD_CHUNK, SSD_GROUPS, SSD_STATE)
    to_end = jnp.exp(cum[:, :, -1:] - cum)
    states = jnp.einsum('bcjgn,bcjgh,bcjghp->bcghpn', bc, to_end, xc)
    chunk_decay = jnp.exp(cum[:, :, -1])

    def step(h, inp):
        dec, st = inp
        return dec[..., None, None] * h + st, h

    h_fin, h_start = lax.scan(step, h0, (jnp.moveaxis(chunk_decay, 1, 0), jnp.moveaxis(states, 1, 0)))
    if cm is None:
        return None, h_fin
    h_start = jnp.moveaxis(h_start, 0, 1)
    cc = cm.astype(jnp.float32).reshape(bsz, nc, SSD_CHUNK, SSD_GROUPS, SSD_STATE)
    seg = cum[:, :, :, None] - cum[:, :, None, :]
    lower = jnp.tril(jnp.ones((SSD_CHUNK, SSD_CHUNK), dtype=bool))[:, :, None, None]
    decay = jnp.exp(jnp.where(lower, seg, -jnp.inf))
    cb = jnp.einsum('bcign,bcjgn->bcijg', cc, bc)
    y = (jnp.einsum('bcijg,bcijgh,bcjghp->bcighp', cb, decay, xc)
         + jnp.einsum('bcign,bcigh,bcghpn->bcighp', cc, jnp.exp(cum), h_start))
    return y.reshape(bsz, t, SSD_HEADS, SSD_HEADDIM), h_fin


def gated_rmsnorm(y, z, w):
    u = (y * jax.nn.silu(z)).astype(jnp.float32)
    ug = u.reshape(*u.shape[:-1], SSD_GROUPS, -1)
    ug = ug * lax.rsqrt(jnp.mean(jnp.square(ug), axis=-1, keepdims=True) + RMS_EPS)
    return (ug.reshape(u.shape) * w.astype(jnp.float32)).astype(y.dtype)


def ssd_branch(xb_raw, c_raw, dt_raw, p, h0_f, h0_b, rows):
    bsz, t, _ = xb_raw.shape
    xb = jax.nn.silu(short_conv(xb_raw, p['ssd_conv_w'][:, :SSD_XB], p['ssd_conv_b'][:SSD_XB], rows))
    xh = xb[..., :SSD_D_INNER].reshape(bsz, t, SSD_HEADS, SSD_HEADDIM)
    bm = xb[..., SSD_D_INNER:].reshape(bsz, t, SSD_GROUPS, SSD_STATE)
    cm = None
    if c_raw is not None:
        cm = jax.nn.silu(short_conv(c_raw, p['ssd_conv_w'][:, SSD_XB:], p['ssd_conv_b'][SSD_XB:], rows))
        cm = cm.reshape(bsz, t, SSD_GROUPS, SSD_STATE)
    dt = jax.nn.softplus(dt_raw.astype(jnp.float32).reshape(bsz, t, 2, SSD_HEADS) + p['ssd_dt_bias'])
    a_neg = -jnp.exp(p['ssd_a_log'].astype(jnp.float32))
    flip = lambda u: None if u is None else jnp.flip(u, axis=1)
    y_f, s_f = ssd_chunked(xh, dt[:, :, 0], a_neg[0], bm, cm, h0_f)
    y_b, s_b = ssd_chunked(flip(xh), flip(dt[:, :, 1]), a_neg[1], flip(bm), flip(cm), h0_b)
    if c_raw is None:
        return None, s_f, s_b
    y = y_f + flip(y_b) + p['ssd_d'][:, None] * xh
    return y.reshape(bsz, t, SSD_D_INNER).astype(xb_raw.dtype), s_f, s_b


def lru_combine(e1, e2):
    a1, b1 = e1
    a2, b2 = e2
    return a1 * a2, a2 * b1 + b2


def rglru(u, wa, ba, wi, bi, lam, h0, reverse):
    bsz, t, w = u.shape
    uf = u.astype(jnp.float32)
    ub = uf.reshape(bsz, t, LRU_BLOCKS, LRU_BLOCK_W)
    r = jax.nn.sigmoid(jnp.einsum('btkc,kcd->btkd', ub, wa).reshape(bsz, t, w) + ba)
    i = jax.nn.sigmoid(jnp.einsum('btkc,kcd->btkd', ub, wi).reshape(bsz, t, w) + bi)
    log_a = -LRU_C * r * jax.nn.softplus(-lam)
    a = jnp.exp(log_a)
    b_in = jnp.sqrt(-jnp.expm1(2.0 * log_a)) * (i * uf)
    edge = t - 1 if reverse else 0
    b_in = b_in.at[:, edge].add(a[:, edge] * h0)
    _, h = lax.associative_scan(lru_combine, (a, b_in), reverse=reverse, axis=1)
    return h, h[:, 0 if reverse else t - 1]


def lru_branch(u_raw, p, h0_f, h0_b, rows, need_y):
    u = short_conv(u_raw, p['lru_conv_w'], p['lru_conv_b'], rows)
    h_f, s_f = rglru(u, p['lru_wa'][0], p['lru_ba'][0], p['lru_wi'][0], p['lru_bi'][0], p['lru_lambda'][0], h0_f, False)
    h_b, s_b = rglru(u, p['lru_wa'][1], p['lru_ba'][1], p['lru_wi'][1], p['lru_bi'][1], p['lru_lambda'][1], h0_b, True)
    if not need_y:
        return None, s_f, s_b
    return (h_f + h_b).astype(u_raw.dtype), s_f, s_b


def token_mixer(h, p, init, rows, need_out):
    cols = IN_COLS if need_out else STATE_COLS
    proj = h @ p['w_in'][:, :cols]
    xb_raw = proj[..., :O_DT]
    dt_raw = proj[..., O_DT:O_LRU]
    lru_raw = proj[..., O_LRU:STATE_COLS]
    c_raw = proj[..., O_C:O_Z] if need_out else None
    y_ssd, s_f, s_b = ssd_branch(xb_raw, c_raw, dt_raw, p, init[0], init[1], rows)
    y_lru, l_f, l_b = lru_branch(lru_raw, p, init[2], init[3], rows, need_out)
    states = (s_f, s_b, l_f, l_b)
    if not need_out:
        return None, states
    z = proj[..., O_Z:O_LRU_GATE]
    lru_gate = proj[..., O_LRU_GATE:O_MERGE]
    gates = jax.nn.sigmoid(proj[..., O_MERGE:] + p['b_gate'])
    g_ssd, g_lru = jnp.split(gates, N_BRANCH, axis=-1)
    br_ssd = gated_rmsnorm(y_ssd, z, p['ssd_norm_w']) @ p['w_br_ssd']
    br_lru = (y_lru * jax.nn.gelu(lru_gate)) @ p['w_br_lru']
    return (g_ssd * br_ssd + g_lru * br_lru) @ p['w_out'], states


def sq_relu_mlp(h, p):
    return jnp.square(jax.nn.relu(h @ p['w_mlp1'] + p['b_mlp1'])) @ p['w_mlp2'] + p['b_mlp2']


def setup_inputs(seed: int = 0) -> dict:
    key = jax.random.key(seed)
    ks = jax.random.split(key, 40)
    f32 = jnp.float32

    def nrm(k, shape, fan_in, gain=1.0):
        return jax.random.normal(k, shape, f32) * (gain * fan_in ** -0.5)

    def small(k, shape):
        return 0.01 * jax.random.normal(k, shape, f32)

    dt0 = jnp.exp(jax.random.uniform(ks[8], (DEPTH, 2, SSD_HEADS), f32, minval=math.log(1e-3), maxval=math.log(1e-1)))
    a_pow = jax.random.uniform(ks[17], (DEPTH, 2, LRU_WIDTH), f32, minval=0.9, maxval=0.999)
    a_base = a_pow ** (1.0 / LRU_C)
    return {
        'x': jax.random.normal(ks[0], (BATCH, SEQ, D_MODEL), f32),
        'c': jax.random.normal(ks[1], (BATCH, D_MODEL), f32),
        'ctx': jax.random.normal(ks[2], (BATCH, CTX_LEN, D_MODEL), f32),
        'c_ctx': jax.random.normal(ks[3], (D_MODEL,), f32),
        'w_mod': nrm(ks[4], (DEPTH, D_MODEL, N_MOD * D_MODEL), D_MODEL),
        'b_mod': small(ks[5], (DEPTH, N_MOD * D_MODEL)),
        'w_in': nrm(ks[6], (DEPTH, D_MODEL, IN_COLS), D_MODEL),
        'b_gate': small(ks[7], (DEPTH, N_BRANCH * D_MODEL)),
        'ssd_conv_w': nrm(ks[9], (DEPTH, CONV_K, SSD_XBC), CONV_K),
        'ssd_conv_b': small(ks[10], (DEPTH, SSD_XBC)),
        'ssd_dt_bias': dt0 + jnp.log(-jnp.expm1(-dt0)),
        'ssd_a_log': jnp.log(jax.random.uniform(ks[11], (DEPTH, 2, SSD_HEADS), f32, minval=1.0, maxval=16.0)),
        'ssd_d': 1.0 + small(ks[12], (DEPTH, SSD_HEADS)),
        'ssd_norm_w': 1.0 + small(ks[13], (DEPTH, SSD_D_INNER)),
        'lru_conv_w': nrm(ks[14], (DEPTH, CONV_K, LRU_WIDTH), CONV_K),
        'lru_conv_b': small(ks[15], (DEPTH, LRU_WIDTH)),
        'lru_wa': nrm(ks[16], (DEPTH, 2, LRU_BLOCKS, LRU_BLOCK_W, LRU_BLOCK_W), LRU_BLOCK_W),
        'lru_ba': small(ks[18], (DEPTH, 2, LRU_WIDTH)),
        'lru_wi': nrm(ks[19], (DEPTH, 2, LRU_BLOCKS, LRU_BLOCK_W, LRU_BLOCK_W), LRU_BLOCK_W),
        'lru_bi': small(ks[20], (DEPTH, 2, LRU_WIDTH)),
        'lru_lambda': jnp.log(a_base) - jnp.log1p(-a_base),
        'w_br_ssd': nrm(ks[21], (DEPTH, SSD_D_INNER, D_MODEL), SSD_D_INNER, DEEPNORM_BETA),
        'w_br_lru': nrm(ks[22], (DEPTH, LRU_WIDTH, D_MODEL), LRU_WIDTH, DEEPNORM_BETA),
        'w_out': nrm(ks[23], (DEPTH, D_MODEL, D_MODEL), D_MODEL, DEEPNORM_BETA),
        'ln1_g': 1.0 + small(ks[24], (DEPTH, D_MODEL)),
        'ln1_b': small(ks[25], (DEPTH, D_MODEL)),
        'w_mlp1': nrm(ks[26], (DEPTH, D_MODEL, MLP_HIDDEN), D_MODEL),
        'b_mlp1': small(ks[27], (DEPTH, MLP_HIDDEN)),
        'w_mlp2': nrm(ks[28], (DEPTH, MLP_HIDDEN, D_MODEL), MLP_HIDDEN, DEEPNORM_BETA),
        'b_mlp2': small(ks[29], (DEPTH, D_MODEL)),
        'ln2_g': 1.0 + small(ks[30], (DEPTH, D_MODEL)),
        'ln2_b': small(ks[31], (DEPTH, D_MODEL)),
    }


def reference(x, c, ctx, c_ctx, w_mod, b_mod, w_in, b_gate, ssd_conv_w, ssd_conv_b, ssd_dt_bias,
              ssd_a_log, ssd_d, ssd_norm_w, lru_conv_w, lru_conv_b, lru_wa, lru_ba, lru_wi, lru_bi,
              lru_lambda, w_br_ssd, w_br_lru, w_out, ln1_g, ln1_b, w_mlp1, b_mlp1, w_mlp2, b_mlp2,
              ln2_g, ln2_b):
    bsz = x.shape[0]
    rows = x.shape[1] // GRID_W
    for l in range(DEPTH):
        p = dict(w_in=w_in[l], b_gate=b_gate[l], ssd_conv_w=ssd_conv_w[l], ssd_conv_b=ssd_conv_b[l],
                 ssd_dt_bias=ssd_dt_bias[l], ssd_a_log=ssd_a_log[l], ssd_d=ssd_d[l], ssd_norm_w=ssd_norm_w[l],
                 lru_conv_w=lru_conv_w[l], lru_conv_b=lru_conv_b[l], lru_wa=lru_wa[l], lru_ba=lru_ba[l],
                 lru_wi=lru_wi[l], lru_bi=lru_bi[l], lru_lambda=lru_lambda[l], w_br_ssd=w_br_ssd[l],
                 w_br_lru=w_br_lru[l], w_out=w_out[l], w_mlp1=w_mlp1[l], b_mlp1=b_mlp1[l],
                 w_mlp2=w_mlp2[l], b_mlp2=b_mlp2[l])
        last = l == DEPTH - 1
        zero_ssd = jnp.zeros((bsz, SSD_GROUPS, SSD_HPG, SSD_HEADDIM, SSD_STATE), jnp.float32)
        zero_lru = jnp.zeros((bsz, LRU_WIDTH), jnp.float32)
        mc = modulation(c_ctx, w_mod[l], b_mod[l], 2 if last else N_MOD)
        ctx_mix, ctx_states = token_mixer(modulate(ctx, mc[0], mc[1]), p,
                                          (zero_ssd, zero_ssd, zero_lru, zero_lru), None, not last)
        mx = [m[:, None, :] for m in modulation(c, w_mod[l], b_mod[l], N_MOD)]
        x_mix, _ = token_mixer(modulate(x, mx[0], mx[1]), p, ctx_states, rows, True)
        x = layer_norm(DEEPNORM_ALPHA * x + mx[2] * x_mix, ln1_g[l], ln1_b[l])
        x = layer_norm(DEEPNORM_ALPHA * x + mx[5] * sq_relu_mlp(modulate(x, mx[3], mx[4]), p), ln2_g[l], ln2_b[l])
        if not last:
            ctx = layer_norm(DEEPNORM_ALPHA * ctx + mc[2] * ctx_mix, ln1_g[l], ln1_b[l])
            ctx = layer_norm(DEEPNORM_ALPHA * ctx + mc[5] * sq_relu_mlp(modulate(ctx, mc[3], mc[4]), p),
                             ln2_g[l], ln2_b[l])
    return x
```

```python
import functools

import jax
import jax.numpy as jnp
from jax import lax
from jax.experimental import pallas as pl
from jax.experimental.pallas import tpu as pltpu

F32 = jnp.float32
BF16 = jnp.bfloat16
HIGHEST = lax.Precision.HIGHEST

D_MODEL = 1024
GRID_W = 64
SSD_D_INNER = 2 * D_MODEL
SSD_HEADDIM = 64
SSD_HEADS = SSD_D_INNER // SSD_HEADDIM
SSD_GROUPS = 8
SSD_HPG = SSD_HEADS // SSD_GROUPS
SSD_STATE = 128
SSD_CHUNK = 128
SSD_GROUP_W = SSD_HPG * SSD_HEADDIM
CONV_K = 4
LRU_WIDTH = D_MODEL
LRU_BLOCKS = 8
LRU_BLOCK_W = LRU_WIDTH // LRU_BLOCKS
LRU_C = 8.0
MLP_HIDDEN = 4 * D_MODEL
N_MOD = 6
DEPTH = 1
DEEPNORM_ALPHA = (2 * DEPTH) ** 0.25
LN_EPS = 1e-6
RMS_EPS = 1e-5

SSD_BC_W = SSD_GROUPS * SSD_STATE
SSD_XB = SSD_D_INNER + SSD_BC_W
SSD_XBC = SSD_D_INNER + 2 * SSD_BC_W
SSD_DT = 2 * SSD_HEADS
O_DT = SSD_XB
O_LRU = O_DT + SSD_DT
STATE_COLS = O_LRU + LRU_WIDTH
O_C = STATE_COLS
O_Z = O_C + SSD_BC_W
O_LRU_GATE = O_Z + SSD_D_INNER
O_MERGE = O_LRU_GATE + LRU_WIDTH

LANES = 128
SUBLANES = 8
VMEM_LIMIT = 56 * 1024 * 1024

PROJ_TN = 512
PROJ_SUB = 256
DT_W = LANES
LRU_TC = 256
LRU_FIX = 64


def _silu(v):
    return v / (1.0 + jnp.exp(-v))


def _sigmoid(v):
    return 1.0 / (1.0 + jnp.exp(-v))


def _softplus(v):
    return jnp.maximum(v, 0.0) + jnp.log1p(jnp.exp(-jnp.abs(v)))


def _layer_norm(v):
    mu = jnp.mean(v, axis=-1, keepdims=True)
    vc = v - mu
    var = jnp.mean(vc * vc, axis=-1, keepdims=True)
    return vc * lax.rsqrt(var + LN_EPS)


def _mod_kernel(c_ref, w_ref, b_ref, o_ref):
    s = _silu(c_ref[...])
    o_ref[...] = jnp.dot(s, w_ref[...], precision=HIGHEST, preferred_element_type=F32) + b_ref[...]


def _modulation(cvecs, w_mod, b_mod):
    rows, ncols = cvecs.shape[0], w_mod.shape[1]
    tn = 512
    return pl.pallas_call(
        _mod_kernel,
        grid=(ncols // tn,),
        in_specs=[pl.BlockSpec((rows, D_MODEL), lambda j: (0, 0)),
                  pl.BlockSpec((D_MODEL, tn), lambda j: (0, j)),
                  pl.BlockSpec((1, tn), lambda j: (0, j))],
        out_specs=pl.BlockSpec((rows, tn), lambda j: (0, j)),
        out_shape=jax.ShapeDtypeStruct((rows, ncols), F32),
        name="modulation",
    )(cvecs, w_mod, b_mod.reshape(1, ncols))


def _inproj_kernel(x_ref, shift_ref, scale_ref, w_ref, cw_ref, cb_ref, wdt_ref, dtb_ref,
                   p_ref, dt_ref, h_scr, *, segs, row_len, tb):
    j = pl.program_id(2)

    @pl.when(j == 0)
    def _():
        for s in range(tb // PROJ_SUB):
            rows = pl.ds(s * PROJ_SUB, PROJ_SUB)
            h = _layer_norm(x_ref[0, rows, :]) * (1.0 + scale_ref[0]) + shift_ref[0]
            h_scr[rows, :] = h.astype(BF16)
            dt_raw = jnp.dot(h, wdt_ref[...], precision=HIGHEST, preferred_element_type=F32)
            dt_ref[0, rows, :] = _softplus(dt_raw + dtb_ref[...])

    def run(kind):
        tn = w_ref.shape[1]
        cw = cw_ref[...]
        cb = cb_ref[...]
        for s in range(tb // PROJ_SUB):
            rows = pl.ds(s * PROJ_SUB, PROJ_SUB)
            acc = jnp.dot(h_scr[rows, :], w_ref[...], preferred_element_type=F32)
            if kind in ("conv_silu", "conv"):
                pos = lax.broadcasted_iota(jnp.int32, (PROJ_SUB, tn), 0) % row_len
                out = cb + acc * cw[2:3]
                out = out + jnp.where(pos >= 2, pltpu.roll(acc, 2, 0), 0.0) * cw[0:1]
                out = out + jnp.where(pos >= 1, pltpu.roll(acc, 1, 0), 0.0) * cw[1:2]
                out = out + jnp.where(pos <= row_len - 2, pltpu.roll(acc, PROJ_SUB - 1, 0), 0.0) * cw[3:4]
                if kind == "conv_silu":
                    out = _silu(out)
            elif kind == "silu":
                out = _silu(acc)
            elif kind == "gelu":
                out = 0.5 * acc * (1.0 + jnp.tanh(0.7978845608028654 * (acc + 0.044715 * acc * acc * acc)))
            elif kind == "sigmoid_bias":
                out = _sigmoid(acc + cb)
            else:
                raise ValueError(kind)
            p_ref[0, rows, :] = out.astype(BF16)

    for lo, hi, kind in segs:
        pl.when((j >= lo) & (j < hi))(functools.partial(run, kind))


def _inproj(xs, shift, scale, w_all, cw_all, cb_all, w_dt, dt_bias, *, segs, row_len, tb):
    bsz, t, _ = xs.shape
    ncols = w_all.shape[1]
    kern = functools.partial(_inproj_kernel, segs=segs, row_len=row_len, tb=tb)
    return pl.pallas_call(
        kern,
        grid=(bsz, t // tb, ncols // PROJ_TN),
        in_specs=[pl.BlockSpec((1, tb, D_MODEL), lambda b, i, j: (b, i, 0)),
                  pl.BlockSpec((1, 1, D_MODEL), lambda b, i, j: (b, 0, 0)),
                  pl.BlockSpec((1, 1, D_MODEL), lambda b, i, j: (b, 0, 0)),
                  pl.BlockSpec((D_MODEL, PROJ_TN), lambda b, i, j: (0, j)),
                  pl.BlockSpec((SUBLANES, PROJ_TN), lambda b, i, j: (0, j)),
                  pl.BlockSpec((1, PROJ_TN), lambda b, i, j: (0, j)),
                  pl.BlockSpec((D_MODEL, DT_W), lambda b, i, j: (0, 0)),
                  pl.BlockSpec((1, DT_W), lambda b, i, j: (0, 0))],
        out_specs=[pl.BlockSpec((1, tb, PROJ_TN), lambda b, i, j: (b, i, j)),
                   pl.BlockSpec((1, tb, DT_W), lambda b, i, j: (b, i, 0))],
        out_shape=[jax.ShapeDtypeStruct((bsz, t, ncols), BF16),
                   jax.ShapeDtypeStruct((bsz, t, DT_W), F32)],
        scratch_shapes=[pltpu.VMEM((tb, D_MODEL), BF16)],
        compiler_params=pltpu.CompilerParams(
            dimension_semantics=("parallel", "parallel", "arbitrary"),
            vmem_limit_bytes=VMEM_LIMIT),
        name="inproj",
    )(xs, shift, scale, w_all, cw_all, cb_all, w_dt, dt_bias)


def _ssd_kernel(*refs, direction, mode, nchunks):
    if mode == "full":
        xbc_ref, dt_ref, alog_ref, h0_ref, dskip_ref, yin_ref, y_ref, s_scr = refs
    elif mode == "inter":
        xbc_ref, dt_ref, alog_ref, h0_ref, y_ref, s_scr = refs
    else:
        xbc_ref, dt_ref, alog_ref, h0_ref, sfin_ref, s_scr = refs
    c = pl.program_id(1)
    L = SSD_CHUNK
    fwd = direction == "fwd"

    @pl.when(c == 0)
    def _():
        s_scr[...] = h0_ref[0]

    dt = dt_ref[0]
    a_neg = -jnp.exp(alog_ref[...])
    dta = dt * a_neg
    ri = lax.broadcasted_iota(jnp.int32, (L, L), 0)
    ci = lax.broadcasted_iota(jnp.int32, (L, L), 1)
    lower = ri >= ci
    upper = ri <= ci
    tril = jnp.where(lower, 1.0, 0.0).astype(F32)
    triu = jnp.where(upper, 1.0, 0.0).astype(F32)
    cum = jnp.dot(tril if fwd else triu, dta, precision=HIGHEST, preferred_element_type=F32)
    total = cum[L - 1:L, :] if fwd else cum[0:1, :]
    w_state = dt * jnp.exp(total - cum)
    e_in = jnp.exp(cum)
    dec_tot = jnp.exp(total)
    hoff = 0 if fwd else SSD_HEADS

    if mode == "full":
        dta_t = dta.T
        dt_t = dt.T
        cum_f = cum
        cum_b = jnp.dot(triu, dta, precision=HIGHEST, preferred_element_type=F32)
        cum_f_t = jnp.dot(dta_t, triu, precision=HIGHEST, preferred_element_type=F32)
        cum_b_t = jnp.dot(dta_t, tril, precision=HIGHEST, preferred_element_type=F32)

    for g in range(SSD_GROUPS):
        gx = slice(g * SSD_GROUP_W, (g + 1) * SSD_GROUP_W)
        bg = xbc_ref[0, :, SSD_D_INNER + g * SSD_STATE:SSD_D_INNER + (g + 1) * SSD_STATE]
        xg = xbc_ref[0, :, gx].astype(F32)
        s_old = s_scr[:, gx]
        wcols, ecols, dcols = [], [], []
        for hh in range(SSD_HPG):
            col = hoff + g * SSD_HPG + hh
            wcols.append(jnp.broadcast_to(w_state[:, col:col + 1], (L, SSD_HEADDIM)))
            ecols.append(jnp.broadcast_to(e_in[:, col:col + 1], (L, SSD_HEADDIM)))
            dcols.append(jnp.broadcast_to(dec_tot[:, col:col + 1], (SSD_STATE, SSD_HEADDIM)))
        w_exp = jnp.concatenate(wcols, axis=1)
        dec_exp = jnp.concatenate(dcols, axis=1)
        xw = (xg * w_exp).astype(BF16)
        upd = lax.dot_general(bg, xw, (((0,), (0,)), ((), ())), preferred_element_type=F32)
        s_scr[:, gx] = dec_exp * s_old + upd

        if mode == "state":
            continue
        e_exp = jnp.concatenate(ecols, axis=1)
        cg = xbc_ref[0, :, SSD_XB + g * SSD_STATE:SSD_XB + (g + 1) * SSD_STATE]
        y_g = e_exp * jnp.dot(cg, s_old.astype(BF16), preferred_element_type=F32)
        if mode == "inter":
            y_ref[0, :, gx] = y_g.astype(y_ref.dtype)
            continue

        y_g = y_g + yin_ref[0, :, gx].astype(F32) + dskip_ref[:, gx] * xg
        cb = lax.dot_general(cg, bg, (((1,), (1,)), ((), ())), preferred_element_type=F32)
        xgb = xbc_ref[0, :, gx]
        heads = []
        for hh in range(SSD_HPG):
            hf = g * SSD_HPG + hh
            hb = SSD_HEADS + hf
            seg_f = cum_f[:, hf:hf + 1] - cum_f_t[hf:hf + 1, :]
            seg_b = cum_b[:, hb:hb + 1] - cum_b_t[hb:hb + 1, :]
            m = (jnp.where(lower, jnp.exp(seg_f), 0.0) * dt_t[hf:hf + 1, :]
                 + jnp.where(upper, jnp.exp(seg_b), 0.0) * dt_t[hb:hb + 1, :])
            m = (cb * m).astype(BF16)
            heads.append(jnp.dot(m, xgb[:, hh * SSD_HEADDIM:(hh + 1) * SSD_HEADDIM],
                                 preferred_element_type=F32))
        y_g = y_g + jnp.concatenate(heads, axis=1)
        y_ref[0, :, gx] = y_g.astype(y_ref.dtype)

    if mode == "state":
        @pl.when(c == nchunks - 1)
        def _():
            sfin_ref[0] = s_scr[...]


def _ssd_sweep(xbc, dt, alog_row, h0, *, direction, mode, xbc_width, dskip=None, yin=None):
    bsz, t, _ = xbc.shape
    nchunks = t // SSD_CHUNK
    hp = SSD_D_INNER
    if direction == "fwd":
        cmap = lambda b, c: (b, c, 0)
    else:
        cmap = lambda b, c: (b, nchunks - 1 - c, 0)
    in_specs = [pl.BlockSpec((1, SSD_CHUNK, xbc_width), cmap),
                pl.BlockSpec((1, SSD_CHUNK, DT_W), cmap),
                pl.BlockSpec((1, DT_W), lambda b, c: (0, 0)),
                pl.BlockSpec((1, SSD_STATE, hp), lambda b, c: (b, 0, 0))]
    args = [xbc, dt, alog_row, h0]
    if mode == "full":
        in_specs += [pl.BlockSpec((1, hp), lambda b, c: (0, 0)),
                     pl.BlockSpec((1, SSD_CHUNK, hp), cmap)]
        args += [dskip, yin]
    if mode == "state":
        out_specs = pl.BlockSpec((1, SSD_STATE, hp), lambda b, c: (b, 0, 0))
        out_shape = jax.ShapeDtypeStruct((bsz, SSD_STATE, hp), F32)
    else:
        out_specs = pl.BlockSpec((1, SSD_CHUNK, hp), cmap)
        out_shape = jax.ShapeDtypeStruct((bsz, t, hp), BF16)
    kern = functools.partial(_ssd_kernel, direction=direction, mode=mode, nchunks=nchunks)
    return pl.pallas_call(
        kern,
        grid=(bsz, nchunks),
        in_specs=in_specs,
        out_specs=out_specs,
        out_shape=out_shape,
        scratch_shapes=[pltpu.VMEM((SSD_STATE, hp), F32)],
        compiler_params=pltpu.CompilerParams(
            dimension_semantics=("parallel", "arbitrary"),
            vmem_limit_bytes=VMEM_LIMIT),
        name=f"ssd_{direction}_{mode}",
    )(*args)


def _lru_kernel(*refs, t, with_out):
    if with_out:
        (u_ref, g_ref, w_ref, gb_ref, lam_ref, h0f_ref, h0b_ref,
         y_ref, a_scr, b_scr, acc_scr) = refs
    else:
        (u_ref, w_ref, gb_ref, lam_ref, h0f_ref, h0b_ref,
         sf_ref, sb_ref, a_scr, b_scr) = refs
    nseg = SUBLANES
    seg_len = t // nseg
    bw = LRU_BLOCK_W

    for d in range(2):
        fwd = d == 0
        lam = lam_ref[d:d + 1, :]
        cneg = -LRU_C * _softplus(-lam)

        def gates(ci, carry, d=d, cneg=cneg):
            rows = pl.ds(pl.multiple_of(ci * LRU_TC, LRU_TC), LRU_TC)
            for k in range(LRU_BLOCKS):
                cols = slice(k * bw, (k + 1) * bw)
                ub = u_ref[0, rows, cols]
                pre = jnp.dot(ub, w_ref[k, :, d * 2 * bw:(d + 1) * 2 * bw],
                              preferred_element_type=F32)
                pre = pre + gb_ref[k:k + 1, d * 2 * bw:(d + 1) * 2 * bw]
                r = _sigmoid(pre[:, :bw])
                ig = _sigmoid(pre[:, bw:])
                log_a = r * cneg[:, cols]
                a = jnp.exp(log_a)
                th = jnp.tanh(log_a)
                mult = jnp.sqrt(-2.0 * th / (1.0 - th))
                a_scr[k, rows, :] = a
                b_scr[k, rows, :] = mult * (ig * ub.astype(F32))
            return carry

        lax.fori_loop(0, t // LRU_TC, gates, 0)

        def scan(i, carry, fwd=fwd):
            pos = i if fwd else seg_len - 1 - i
            idx = pl.ds(pos, nseg, stride=seg_len)
            out = []
            for k in range(LRU_BLOCKS):
                h, p = carry[k]
                a = a_scr[k, idx, :]
                h = a * h + b_scr[k, idx, :]
                p = a * p
                b_scr[k, idx, :] = h
                a_scr[k, idx, :] = p
                out.append((h, p))
            return tuple(out)

        init = tuple((jnp.zeros((nseg, bw), F32), jnp.ones((nseg, bw), F32)) for _ in range(LRU_BLOCKS))
        ends = lax.fori_loop(0, seg_len, scan, init)

        h0 = (h0f_ref if fwd else h0b_ref)[0]
        order = list(range(nseg)) if fwd else list(range(nseg - 1, -1, -1))
        h_in = [[None] * nseg for _ in range(LRU_BLOCKS)]
        for k in range(LRU_BLOCKS):
            cols = slice(k * bw, (k + 1) * bw)
            h_end, p_end = ends[k]
            carry = h0[:, cols]
            for s in order:
                h_in[k][s] = carry
                carry = p_end[s:s + 1, :] * carry + h_end[s:s + 1, :]
            if not with_out:
                (sf_ref if fwd else sb_ref)[0, :, cols] = carry
        if not with_out:
            continue

        for s in range(nseg):
            def fix(bi, c2, s=s, fwd=fwd):
                rows = pl.ds(pl.multiple_of(s * seg_len + bi * LRU_FIX, LRU_FIX), LRU_FIX)
                for k in range(LRU_BLOCKS):
                    cols = slice(k * bw, (k + 1) * bw)
                    hv = b_scr[k, rows, :] + a_scr[k, rows, :] * h_in[k][s]
                    if fwd:
                        acc_scr[k, rows, :] = hv
                    else:
                        gate = g_ref[0, rows, cols].astype(F32)
                        y_ref[0, rows, cols] = ((acc_scr[k, rows, :] + hv) * gate).astype(y_ref.dtype)
                return c2
            lax.fori_loop(0, seg_len // LRU_FIX, fix, 0)


def _lru(u_src, u_col, g_src, g_col, w_blk, gb_blk, lam, h0f, h0b, *, with_out):
    bsz, t, _ = u_src.shape
    w = LRU_WIDTH
    in_specs = [pl.BlockSpec((1, t, w), lambda b: (b, 0, u_col))]
    args = [u_src]
    if with_out:
        in_specs.append(pl.BlockSpec((1, t, w), lambda b: (b, 0, g_col)))
        args.append(g_src)
    in_specs += [pl.BlockSpec((LRU_BLOCKS, LRU_BLOCK_W, 4 * LRU_BLOCK_W), lambda b: (0, 0, 0)),
                 pl.BlockSpec((LRU_BLOCKS, 4 * LRU_BLOCK_W), lambda b: (0, 0)),
                 pl.BlockSpec((2, w), lambda b: (0, 0)),
                 pl.BlockSpec((1, 1, w), lambda b: (b, 0, 0)),
                 pl.BlockSpec((1, 1, w), lambda b: (b, 0, 0))]
    args += [w_blk, gb_blk, lam, h0f, h0b]
    blk_shape = (LRU_BLOCKS, t, LRU_BLOCK_W)
    scratch = [pltpu.VMEM(blk_shape, F32), pltpu.VMEM(blk_shape, F32)]
    if with_out:
        out_specs = pl.BlockSpec((1, t, w), lambda b: (b, 0, 0))
        out_shape = jax.ShapeDtypeStruct((bsz, t, w), BF16)
        scratch.append(pltpu.VMEM(blk_shape, F32))
    else:
        out_specs = [pl.BlockSpec((1, 1, w), lambda b: (b, 0, 0)),
                     pl.BlockSpec((1, 1, w), lambda b: (b, 0, 0))]
        out_shape = [jax.ShapeDtypeStruct((bsz, 1, w), F32),
                     jax.ShapeDtypeStruct((bsz, 1, w), F32)]
    kern = functools.partial(_lru_kernel, t=t, with_out=with_out)
    return pl.pallas_call(
        kern,
        grid=(bsz,),
        in_specs=in_specs,
        out_specs=out_specs,
        out_shape=out_shape,
        scratch_shapes=scratch,
        compiler_params=pltpu.CompilerParams(
            dimension_semantics=("parallel",),
            vmem_limit_bytes=VMEM_LIMIT),
        name="lru_out" if with_out else "lru_state",
    )(*args)


def _post_kernel(x_ref, y_ref, z_ref, gate_ref, ylru_ref, mod_ref, nw_ref, wbs_ref, wbl_ref, wout_ref,
                 ln1g_ref, ln1b_ref, w1_ref, b1_ref, w2_ref, b2_ref, ln2g_ref, ln2b_ref, o_ref):
    d = D_MODEL
    x = x_ref[0]
    u = y_ref[0].astype(F32) * z_ref[0].astype(F32)
    parts = []
    for g in range(SSD_GROUPS):
        ug = u[:, g * SSD_GROUP_W:(g + 1) * SSD_GROUP_W]
        ms = jnp.mean(ug * ug, axis=-1, keepdims=True)
        parts.append(ug * lax.rsqrt(ms + RMS_EPS))
    un = (jnp.concatenate(parts, axis=1) * nw_ref[...]).astype(BF16)
    br_ssd = jnp.dot(un, wbs_ref[...], preferred_element_type=F32)
    br_lru = jnp.dot(ylru_ref[0], wbl_ref[...], preferred_element_type=F32)
    gates = gate_ref[0].astype(F32)
    merged = (gates[:, :d] * br_ssd + gates[:, d:] * br_lru).astype(BF16)
    x_mix = jnp.dot(merged, wout_ref[...], preferred_element_type=F32)
    mod = mod_ref[0]
    gate1 = mod[:, 2 * d:3 * d]
    shift2 = mod[:, 3 * d:4 * d]
    scale2 = mod[:, 4 * d:5 * d]
    gate2 = mod[:, 5 * d:6 * d]
    x1 = _layer_norm(DEEPNORM_ALPHA * x + gate1 * x_mix) * ln1g_ref[...] + ln1b_ref[...]
    h2 = (_layer_norm(x1) * (1.0 + scale2) + shift2).astype(BF16)
    hid = jnp.dot(h2, w1_ref[...], preferred_element_type=F32) + b1_ref[...]
    hid = jnp.maximum(hid, 0.0)
    hid = (hid * hid).astype(BF16)
    mlp = jnp.dot(hid, w2_ref[...], preferred_element_type=F32) + b2_ref[...]
    o_ref[0] = _layer_norm(DEEPNORM_ALPHA * x1 + gate2 * mlp) * ln2g_ref[...] + ln2b_ref[...]


def _const_spec(shape):
    nd = len(shape)
    return pl.BlockSpec(shape, lambda b, i: (0,) * nd, pipeline_mode=pl.Buffered(1))


def _post(x, y_ssd, proj, y_lru, mod, norm_w, w_br_ssd, w_br_lru, w_out, ln1_g, ln1_b,
          w1, b1, w2, b2, ln2_g, ln2_b, *, tb, z_col, gate_col):
    bsz, t, d = x.shape
    di = SSD_D_INNER
    row = lambda v: v.reshape(1, -1)
    return pl.pallas_call(
        _post_kernel,
        grid=(bsz, t // tb),
        in_specs=[pl.BlockSpec((1, tb, d), lambda b, i: (b, i, 0)),
                  pl.BlockSpec((1, tb, di), lambda b, i: (b, i, 0)),
                  pl.BlockSpec((1, tb, di), lambda b, i: (b, i, z_col)),
                  pl.BlockSpec((1, tb, di), lambda b, i: (b, i, gate_col)),
                  pl.BlockSpec((1, tb, d), lambda b, i: (b, i, 0)),
                  pl.BlockSpec((1, 1, N_MOD * d), lambda b, i: (b, 0, 0)),
                  _const_spec((1, di)),
                  _const_spec((di, d)),
                  _const_spec((d, d)),
                  _const_spec((d, d)),
                  _const_spec((1, d)),
                  _const_spec((1, d)),
                  _const_spec((d, MLP_HIDDEN)),
                  _const_spec((1, MLP_HIDDEN)),
                  _const_spec((MLP_HIDDEN, d)),
                  _const_spec((1, d)),
                  _const_spec((1, d)),
                  _const_spec((1, d))],
        out_specs=pl.BlockSpec((1, tb, d), lambda b, i: (b, i, 0)),
        out_shape=jax.ShapeDtypeStruct((bsz, t, d), F32),
        compiler_params=pltpu.CompilerParams(
            dimension_semantics=("parallel", "parallel"),
            vmem_limit_bytes=VMEM_LIMIT),
        name="post",
    )(x, y_ssd, proj, proj, y_lru, mod, row(norm_w), w_br_ssd.astype(BF16), w_br_lru.astype(BF16),
      w_out.astype(BF16), row(ln1_g), row(ln1_b), w1.astype(BF16), row(b1), w2.astype(BF16), row(b2),
      row(ln2_g), row(ln2_b))


def _pad_rows(v, n):
    return jnp.pad(v, ((0, n - v.shape[0]), (0, 0)))


def kernel(x, c, ctx, c_ctx, w_mod, b_mod, w_in, b_gate, ssd_conv_w, ssd_conv_b, ssd_dt_bias, ssd_a_log,
           ssd_d, ssd_norm_w, lru_conv_w, lru_conv_b, lru_wa, lru_ba, lru_wi, lru_bi, lru_lambda, w_br_ssd,
           w_br_lru, w_out, ln1_g, ln1_b, w_mlp1, b_mlp1, w_mlp2, b_mlp2, ln2_g, ln2_b):
    bsz, t, d = x.shape
    tctx = ctx.shape[1]
    l = 0
    wi = w_in[l]

    nrow = -(-(bsz + 1) // SUBLANES) * SUBLANES
    cvecs = _pad_rows(jnp.concatenate([c, c_ctx[None, :]], axis=0), nrow)
    mods = _modulation(cvecs, w_mod[l], b_mod[l])
    mod_x = mods[:bsz].reshape(bsz, 1, N_MOD * d)
    shift_x, scale_x = mod_x[:, :, :d], mod_x[:, :, d:2 * d]
    shift_c = jnp.broadcast_to(mods[bsz:bsz + 1, :d].reshape(1, 1, d), (bsz, 1, d))
    scale_c = jnp.broadcast_to(mods[bsz:bsz + 1, d:2 * d].reshape(1, 1, d), (bsz, 1, d))

    zc = lambda n: jnp.zeros((CONV_K, n), F32)
    z1 = lambda n: jnp.zeros((n,), F32)
    w_x = jnp.concatenate([wi[:, :SSD_XB], wi[:, O_C:O_Z], wi[:, O_Z:O_LRU_GATE], wi[:, O_MERGE:],
                           wi[:, O_LRU:STATE_COLS], wi[:, O_LRU_GATE:O_MERGE]], axis=1).astype(BF16)
    cw_x = _pad_rows(jnp.concatenate([ssd_conv_w[l], zc(SSD_D_INNER), zc(2 * d), lru_conv_w[l], zc(LRU_WIDTH)],
                                     axis=1), SUBLANES)
    cb_x = jnp.concatenate([ssd_conv_b[l], z1(SSD_D_INNER), b_gate[l], lru_conv_b[l], z1(LRU_WIDTH)])[None, :]
    n_xbc, n_z, n_m, n_l = SSD_XBC // PROJ_TN, SSD_D_INNER // PROJ_TN, 2 * d // PROJ_TN, LRU_WIDTH // PROJ_TN
    segs_x, lo = [], 0
    for n, kind in ((n_xbc, "conv_silu"), (n_z, "silu"), (n_m, "sigmoid_bias"), (n_l, "conv"), (n_l, "gelu")):
        segs_x.append((lo, lo + n, kind))
        lo += n
    z_col = SSD_XBC // SSD_D_INNER
    gate_col = z_col + 1
    lru_col = (SSD_XBC + 2 * SSD_D_INNER) // LRU_WIDTH
    lrug_col = lru_col + 1

    w_c = jnp.concatenate([wi[:, :SSD_XB], wi[:, O_LRU:STATE_COLS]], axis=1).astype(BF16)
    cw_c = _pad_rows(jnp.concatenate([ssd_conv_w[l][:, :SSD_XB], lru_conv_w[l]], axis=1), SUBLANES)
    cb_c = jnp.concatenate([ssd_conv_b[l][:SSD_XB], lru_conv_b[l]])[None, :]
    segs_c = [(0, SSD_XB // PROJ_TN, "conv_silu"), (SSD_XB // PROJ_TN, SSD_XB // PROJ_TN + n_l, "conv")]
    lru_col_c = SSD_XB // LRU_WIDTH

    w_dt = jnp.pad(wi[:, O_DT:O_LRU], ((0, 0), (0, DT_W - SSD_DT)))
    dt_bias = jnp.pad(ssd_dt_bias[l].reshape(1, SSD_DT), ((0, 0), (0, DT_W - SSD_DT)))
    alog_row = jnp.pad(ssd_a_log[l].reshape(1, SSD_DT), ((0, 0), (0, DT_W - SSD_DT)))
    dskip = jnp.repeat(ssd_d[l], SSD_HEADDIM)[None, :]

    w_blk = jnp.concatenate([lru_wa[l, 0], lru_wi[l, 0], lru_wa[l, 1], lru_wi[l, 1]], axis=-1).astype(BF16)
    blk = lambda v: v.reshape(LRU_BLOCKS, LRU_BLOCK_W)
    gb_blk = jnp.concatenate([blk(lru_ba[l, 0]), blk(lru_bi[l, 0]), blk(lru_ba[l, 1]), blk(lru_bi[l, 1])], axis=-1)
    lam = lru_lambda[l]

    proj_c, dt_c = _inproj(ctx, shift_c, scale_c, w_c, cw_c, cb_c, w_dt, dt_bias,
                           segs=segs_c, row_len=tctx, tb=tctx)
    zero_s = jnp.zeros((bsz, SSD_STATE, SSD_D_INNER), F32)
    zero_l = jnp.zeros((bsz, 1, LRU_WIDTH), F32)
    s_f = _ssd_sweep(proj_c, dt_c, alog_row, zero_s, direction="fwd", mode="state", xbc_width=SSD_XB)
    s_b = _ssd_sweep(proj_c, dt_c, alog_row, zero_s, direction="bwd", mode="state", xbc_width=SSD_XB)
    l_f, l_b = _lru(proj_c, lru_col_c, None, None, w_blk, gb_blk, lam, zero_l, zero_l, with_out=False)

    proj_x, dt_x = _inproj(x, shift_x, scale_x, w_x, cw_x, cb_x, w_dt, dt_bias,
                           segs=segs_x, row_len=GRID_W, tb=min(t, 1024))
    y_b = _ssd_sweep(proj_x, dt_x, alog_row, s_b, direction="bwd", mode="inter", xbc_width=SSD_XBC)
    y_ssd = _ssd_sweep(proj_x, dt_x, alog_row, s_f, direction="fwd", mode="full", xbc_width=SSD_XBC,
                       dskip=dskip, yin=y_b)
    y_lru = _lru(proj_x, lru_col, proj_x, lrug_col, w_blk, gb_blk, lam, l_f, l_b, with_out=True)
    return _post(x, y_ssd, proj_x, y_lru, mod_x, ssd_norm_w[l], w_br_ssd[l], w_br_lru[l], w_out[l],
                 ln1_g[l], ln1_b[l], w_mlp1[l], b_mlp1[l], w_mlp2[l], b_mlp2[l], ln2_g[l], ln2_b[l],
                 tb=min(t, 256), z_col=z_col, gate_col=gate_col)
```

```python
import functools

import jax
import jax.numpy as jnp
from jax import lax
from jax.experimental import pallas as pl
from jax.experimental.pallas import tpu as pltpu

F32 = jnp.float32
BF16 = jnp.bfloat16
HIGHEST = lax.Precision.HIGHEST

D_MODEL = 1024
GRID_W = 64
SSD_D_INNER = 2 * D_MODEL
SSD_HEADDIM = 64
SSD_HEADS = SSD_D_INNER // SSD_HEADDIM
SSD_GROUPS = 8
SSD_HPG = SSD_HEADS // SSD_GROUPS
SSD_STATE = 128
SSD_CHUNK = 128
SSD_GROUP_W = SSD_HPG * SSD_HEADDIM
CONV_K = 4
LRU_WIDTH = D_MODEL
LRU_BLOCKS = 8
LRU_BLOCK_W = LRU_WIDTH // LRU_BLOCKS
LRU_C = 8.0
MLP_HIDDEN = 4 * D_MODEL
N_MOD = 6
DEPTH = 1
DEEPNORM_ALPHA = (2 * DEPTH) ** 0.25
LN_EPS = 1e-6
RMS_EPS = 1e-5

SSD_BC_W = SSD_GROUPS * SSD_STATE
SSD_XB = SSD_D_INNER + SSD_BC_W
SSD_XBC = SSD_D_INNER + 2 * SSD_BC_W
SSD_DT = 2 * SSD_HEADS
O_DT = SSD_XB
O_LRU = O_DT + SSD_DT
STATE_COLS = O_LRU + LRU_WIDTH
O_C = STATE_COLS
O_Z = O_C + SSD_BC_W
O_LRU_GATE = O_Z + SSD_D_INNER
O_MERGE = O_LRU_GATE + LRU_WIDTH

LANES = 128
SUBLANES = 8
VMEM_LIMIT = 56 * 1024 * 1024

PROJ_TN = 512
PROJ_SUB = 256
PROJ_TB = 512
CONV_GAP = SUBLANES
DT_W = LANES
SSD_TB = 512
LRU_TC = 128
POST_TB = 256


def _silu(v):
    return v / (1.0 + jnp.exp(-v))


def _sigmoid(v):
    return 1.0 / (1.0 + jnp.exp(-v))


def _sigmoid_tanh(v):
    return 0.5 * jnp.tanh(0.5 * v) + 0.5


def _softplus(v):
    return jnp.maximum(v, 0.0) + jnp.log1p(jnp.exp(-jnp.abs(v)))


def _layer_norm(v):
    mu = jnp.mean(v, axis=-1, keepdims=True)
    vc = v - mu
    var = jnp.mean(vc * vc, axis=-1, keepdims=True)
    return vc * lax.rsqrt(var + LN_EPS)


def _group_roll(v, shift):
    rows, width = v.shape
    v3 = v.reshape(rows // SUBLANES, SUBLANES, width)
    return pltpu.roll(v3, shift, 1).reshape(rows, width)


def _split_bf16(v):
    hi = v.astype(BF16)
    lo = (v - hi.astype(F32)).astype(BF16)
    return hi, lo


def _mod_kernel(c_ref, w_ref, b_ref, o_ref):
    s = _silu(c_ref[...])
    o_ref[...] = jnp.dot(s, w_ref[...], precision=HIGHEST, preferred_element_type=F32) + b_ref[...]


def _modulation(cvecs, w_mod, b_mod):
    rows, ncols = cvecs.shape[0], w_mod.shape[1]
    tn = 512
    return pl.pallas_call(
        _mod_kernel,
        grid=(ncols // tn,),
        in_specs=[pl.BlockSpec((rows, D_MODEL), lambda j: (0, 0)),
                  pl.BlockSpec((D_MODEL, tn), lambda j: (0, j)),
                  pl.BlockSpec((1, tn), lambda j: (0, j))],
        out_specs=pl.BlockSpec((rows, tn), lambda j: (0, j)),
        out_shape=jax.ShapeDtypeStruct((rows, ncols), F32),
        name="modulation",
    )(cvecs, w_mod, b_mod.reshape(1, ncols))


def _inproj_kernel(x_ref, shift_ref, scale_ref, w_ref, cw_ref, cb_ref, wdt_ref, dtb_ref,
                   p_ref, dt_ref, h_scr, c_scr, *, segs, row_len, tb):
    sub = PROJ_SUB
    npiece = sub // row_len
    pitch = row_len + CONV_GAP
    zgap = jnp.zeros((CONV_GAP, PROJ_TN), F32)
    for r in range(npiece + 1):
        c_scr[r * pitch:r * pitch + CONV_GAP, :] = zgap

    def sub_block(s, carry):
        rows = pl.ds(pl.multiple_of(s * sub, sub), sub)
        h = _layer_norm(x_ref[0, rows, :]) * (1.0 + scale_ref[0]) + shift_ref[0]
        hb = h.astype(BF16)
        h_scr[...] = hb
        dt2 = jnp.dot(hb, wdt_ref[...], preferred_element_type=F32)
        dt_ref[0, rows, :] = _softplus(dt2[:, :DT_W] + dt2[:, DT_W:] + dtb_ref[...])
        for lo, hi, kind in segs:
            for j in range(lo, hi):
                cols = slice(j * PROJ_TN, (j + 1) * PROJ_TN)
                acc = jnp.dot(h_scr[...], w_ref[:, cols], preferred_element_type=F32)
                cb = cb_ref[:, cols]
                if kind in ("conv_silu", "conv"):
                    cw = cw_ref[:, cols]
                    for r in range(npiece):
                        base = CONV_GAP + r * pitch
                        c_scr[base:base + row_len, :] = acc[r * row_len:(r + 1) * row_len, :]
                    pieces = []
                    for r in range(npiece):
                        base = CONV_GAP + r * pitch
                        o = cb + acc[r * row_len:(r + 1) * row_len, :] * cw[2:3]
                        o = o + c_scr[base - 2:base - 2 + row_len, :] * cw[0:1]
                        o = o + c_scr[base - 1:base - 1 + row_len, :] * cw[1:2]
                        o = o + c_scr[base + 1:base + 1 + row_len, :] * cw[3:4]
                        pieces.append(o)
                    out = pieces[0] if npiece == 1 else jnp.concatenate(pieces, axis=0)
                    if kind == "conv_silu":
                        out = _silu(out)
                elif kind == "silu":
                    out = _silu(acc)
                elif kind == "gelu":
                    out = 0.5 * acc * (1.0 + jnp.tanh(0.7978845608028654 * (acc + 0.044715 * acc * acc * acc)))
                elif kind == "sigmoid_bias":
                    out = _sigmoid(acc + cb)
                else:
                    raise ValueError(kind)
                p_ref[0, rows, cols] = out.astype(BF16)
        return carry

    lax.fori_loop(0, tb // sub, sub_block, 0)


def _inproj(xs, shift, scale, w_all, cw_all, cb_all, w_dt2, dt_bias, *, segs, row_len, tb):
    bsz, t, _ = xs.shape
    ncols = w_all.shape[1]
    kern = functools.partial(_inproj_kernel, segs=segs, row_len=row_len, tb=tb)
    const = lambda shape: pl.BlockSpec(shape, lambda b, i: (0, 0), pipeline_mode=pl.Buffered(1))
    conv_rows = CONV_GAP + (PROJ_SUB // row_len) * (row_len + CONV_GAP)
    return pl.pallas_call(
        kern,
        grid=(bsz, t // tb),
        in_specs=[pl.BlockSpec((1, tb, D_MODEL), lambda b, i: (b, i, 0)),
                  pl.BlockSpec((1, 1, D_MODEL), lambda b, i: (b, 0, 0)),
                  pl.BlockSpec((1, 1, D_MODEL), lambda b, i: (b, 0, 0)),
                  const((D_MODEL, ncols)),
                  const((SUBLANES, ncols)),
                  const((1, ncols)),
                  const((D_MODEL, 2 * DT_W)),
                  const((1, DT_W))],
        out_specs=[pl.BlockSpec((1, tb, ncols), lambda b, i: (b, i, 0)),
                   pl.BlockSpec((1, tb, DT_W), lambda b, i: (b, i, 0))],
        out_shape=[jax.ShapeDtypeStruct((bsz, t, ncols), BF16),
                   jax.ShapeDtypeStruct((bsz, t, DT_W), F32)],
        scratch_shapes=[pltpu.VMEM((PROJ_SUB, D_MODEL), BF16),
                        pltpu.VMEM((conv_rows, PROJ_TN), F32)],
        compiler_params=pltpu.CompilerParams(
            dimension_semantics=("parallel", "parallel"),
            vmem_limit_bytes=VMEM_LIMIT),
        name="inproj",
    )(xs, shift, scale, w_all, cw_all, cb_all, w_dt2, dt_bias)


def _ssd_kernel(*refs, direction, mode, nsteps, nsub):
    if mode == "full":
        xbc_ref, dt_ref, alog_ref, h0_ref, hexp_ref, dskip_ref, yin_ref, y_ref, s_scr, e_scr, y_scr = refs
    elif mode == "inter":
        xbc_ref, dt_ref, alog_ref, h0_ref, hexp_ref, y_ref, s_scr, e_scr = refs
    else:
        xbc_ref, dt_ref, alog_ref, h0_ref, hexp_ref, sfin_ref, s_scr, e_scr = refs
    step = pl.program_id(1)
    L = SSD_CHUNK
    fwd = direction == "fwd"
    groups = [slice(g * SSD_GROUP_W, (g + 1) * SSD_GROUP_W) for g in range(SSD_GROUPS)]
    bcols = [slice(SSD_D_INNER + g * SSD_STATE, SSD_D_INNER + (g + 1) * SSD_STATE) for g in range(SSD_GROUPS)]
    ccols = [slice(SSD_XB + g * SSD_STATE, SSD_XB + (g + 1) * SSD_STATE) for g in range(SSD_GROUPS)]

    @pl.when(step == 0)
    def _():
        s_scr[...] = h0_ref[0]

    a_neg = -jnp.exp(alog_ref[...])
    ri = lax.broadcasted_iota(jnp.int32, (L, L), 0)
    ci = lax.broadcasted_iota(jnp.int32, (L, L), 1)
    lower = ri >= ci
    upper = ri <= ci
    tril = jnp.where(lower, 1.0, 0.0).astype(F32)
    triu = jnp.where(upper, 1.0, 0.0).astype(F32)

    def chunk(i, carry):
        cidx = i if fwd else nsub - 1 - i
        rows = pl.ds(pl.multiple_of(cidx * L, L), L)
        dt = dt_ref[0, rows, :]
        dta = dt * a_neg
        cum = jnp.dot(tril if fwd else triu, dta, precision=HIGHEST, preferred_element_type=F32)
        total = cum[L - 1:L, :] if fwd else cum[0:1, :]
        w_state = dt * jnp.exp(total - cum)
        dec_tot = jnp.broadcast_to(jnp.exp(total), (SUBLANES, DT_W))
        if mode == "state":
            stack = jnp.concatenate([w_state, dec_tot], axis=0)
        else:
            e_in = jnp.exp(cum)
            stack = jnp.concatenate([w_state, dec_tot, e_in], axis=0)
        st_hi, st_lo = _split_bf16(stack)
        e_scr[...] = jnp.dot(jnp.concatenate([st_hi, st_lo], axis=1), hexp_ref[...], preferred_element_type=F32)
        r_dec = L
        r_ein = L + SUBLANES

        if mode != "state":
            for g, gx in enumerate(groups):
                cg = xbc_ref[0, rows, ccols[g]]
                y_g = e_scr[r_ein:r_ein + L, gx] * jnp.dot(cg, s_scr[:, gx].astype(BF16),
                                                           preferred_element_type=F32)
                if mode == "inter":
                    y_ref[0, rows, gx] = y_g.astype(y_ref.dtype)
                else:
                    xg = xbc_ref[0, rows, gx].astype(F32)
                    y_scr[:, gx] = y_g + yin_ref[0, rows, gx].astype(F32) + dskip_ref[:, gx] * xg

        for g, gx in enumerate(groups):
            bg = xbc_ref[0, rows, bcols[g]]
            xw = (xbc_ref[0, rows, gx].astype(F32) * e_scr[0:L, gx]).astype(BF16)
            upd = lax.dot_general(bg, xw, (((0,), (0,)), ((), ())), preferred_element_type=F32)
            s_scr[:, gx] = e_scr[r_dec:r_dec + 1, gx] * s_scr[:, gx] + upd

        if mode == "full":
            dta_t = dta.T
            dt_t = dt.T
            cum_f = cum
            cum_b = jnp.dot(triu, dta, precision=HIGHEST, preferred_element_type=F32)
            cum_f_t = jnp.dot(dta_t, triu, precision=HIGHEST, preferred_element_type=F32)
            cum_b_t = jnp.dot(dta_t, tril, precision=HIGHEST, preferred_element_type=F32)
            for g, gx in enumerate(groups):
                bg = xbc_ref[0, rows, bcols[g]]
                cg = xbc_ref[0, rows, ccols[g]]
                xgb = xbc_ref[0, rows, gx]
                cb = lax.dot_general(cg, bg, (((1,), (1,)), ((), ())), preferred_element_type=F32)
                heads = []
                for hh in range(SSD_HPG):
                    hf = g * SSD_HPG + hh
                    hb = SSD_HEADS + hf
                    seg_f = cum_f[:, hf:hf + 1] - cum_f_t[hf:hf + 1, :]
                    seg_b = cum_b[:, hb:hb + 1] - cum_b_t[hb:hb + 1, :]
                    m = (jnp.where(lower, jnp.exp(seg_f), 0.0) * dt_t[hf:hf + 1, :]
                         + jnp.where(upper, jnp.exp(seg_b), 0.0) * dt_t[hb:hb + 1, :])
                    m = (cb * m).astype(BF16)
                    heads.append(jnp.dot(m, xgb[:, hh * SSD_HEADDIM:(hh + 1) * SSD_HEADDIM],
                                         preferred_element_type=F32))
                y_ref[0, rows, gx] = (y_scr[:, gx] + jnp.concatenate(heads, axis=1)).astype(y_ref.dtype)
        return carry

    lax.fori_loop(0, nsub, chunk, 0)

    if mode == "state":
        @pl.when(step == nsteps - 1)
        def _():
            sfin_ref[0] = s_scr[...]


def _ssd_sweep(xbc, dt, alog_row, h0, hexp, *, direction, mode, xbc_width, dskip=None, yin=None):
    bsz, t, _ = xbc.shape
    tb = min(t, SSD_TB)
    nsteps = t // tb
    hp = SSD_D_INNER
    if direction == "fwd":
        cmap = lambda b, c: (b, c, 0)
    else:
        cmap = lambda b, c: (b, nsteps - 1 - c, 0)
    const = lambda shape: pl.BlockSpec(shape, lambda b, c: (0, 0))
    in_specs = [pl.BlockSpec((1, tb, xbc_width), cmap),
                pl.BlockSpec((1, tb, DT_W), cmap),
                const((1, DT_W)),
                pl.BlockSpec((1, SSD_STATE, hp), lambda b, c: (b, 0, 0)),
                const((2 * DT_W, hp))]
    args = [xbc, dt, alog_row, h0, jnp.concatenate([hexp, hexp], axis=0)]
    stack_rows = (SSD_CHUNK + SUBLANES) if mode == "state" else (2 * SSD_CHUNK + SUBLANES)
    scratch = [pltpu.VMEM((SSD_STATE, hp), F32), pltpu.VMEM((stack_rows, hp), F32)]
    if mode == "full":
        scratch.append(pltpu.VMEM((SSD_CHUNK, hp), F32))
    if mode == "full":
        in_specs += [const((1, hp)), pl.BlockSpec((1, tb, hp), cmap)]
        args += [dskip, yin]
    if mode == "state":
        out_specs = pl.BlockSpec((1, SSD_STATE, hp), lambda b, c: (b, 0, 0))
        out_shape = jax.ShapeDtypeStruct((bsz, SSD_STATE, hp), F32)
    else:
        out_specs = pl.BlockSpec((1, tb, hp), cmap)
        out_shape = jax.ShapeDtypeStruct((bsz, t, hp), BF16)
    kern = functools.partial(_ssd_kernel, direction=direction, mode=mode, nsteps=nsteps, nsub=tb // SSD_CHUNK)
    return pl.pallas_call(
        kern,
        grid=(bsz, nsteps),
        in_specs=in_specs,
        out_specs=out_specs,
        out_shape=out_shape,
        scratch_shapes=scratch,
        compiler_params=pltpu.CompilerParams(
            dimension_semantics=("parallel", "arbitrary"),
            vmem_limit_bytes=VMEM_LIMIT),
        name=f"ssd_{direction}_{mode}",
    )(*args)


def _lru_kernel(*refs, t, with_out):
    if with_out:
        u_ref, g_ref, w_ref, gb_ref, lam_ref, h0f_ref, h0b_ref, y_ref, acc_scr = refs
    else:
        u_ref, w_ref, gb_ref, lam_ref, h0f_ref, h0b_ref, sf_ref, sb_ref = refs
    bw = LRU_BLOCK_W
    tc = LRU_TC
    nchunk = t // tc
    ngrp = tc // SUBLANES
    sub_idx = lax.broadcasted_iota(jnp.int32, (tc, bw), 0) % SUBLANES

    for d in range(2):
        fwd = d == 0
        cneg = -LRU_C * _softplus(-lam_ref[d:d + 1, :])
        h0 = (h0f_ref if fwd else h0b_ref)[0]
        oks = {s: (sub_idx >= s) if fwd else (sub_idx < SUBLANES - s) for s in (1, 2, 4)}

        def chunk(i, carry, d=d, fwd=fwd, cneg=cneg):
            cidx = i if fwd else nchunk - 1 - i
            rows = pl.ds(pl.multiple_of(cidx * tc, tc), tc)
            new_carry = []
            for k in range(LRU_BLOCKS):
                cols = slice(k * bw, (k + 1) * bw)
                ub = u_ref[0, rows, cols]
                pre = jnp.dot(ub, w_ref[k, :, d * 2 * bw:(d + 1) * 2 * bw], preferred_element_type=F32)
                pre = pre + gb_ref[k:k + 1, d * 2 * bw:(d + 1) * 2 * bw]
                r = _sigmoid_tanh(pre[:, :bw])
                ig = _sigmoid_tanh(pre[:, bw:])
                log_a = r * cneg[:, cols]
                a = jnp.exp(log_a)
                th = jnp.tanh(log_a)
                b = jnp.sqrt(-2.0 * th / (1.0 - th)) * (ig * ub.astype(F32))
                for s in (1, 2, 4):
                    ok = oks[s]
                    shift = s if fwd else SUBLANES - s
                    b = b + jnp.where(ok, a, 0.0) * _group_roll(b, shift)
                    a = a * jnp.where(ok, _group_roll(a, shift), 1.0)
                c = carry[k]
                hs = [None] * ngrp
                order = range(ngrp) if fwd else range(ngrp - 1, -1, -1)
                last = SUBLANES - 1 if fwd else 0
                for gi in order:
                    sl = slice(gi * SUBLANES, (gi + 1) * SUBLANES)
                    hg = b[sl, :] + a[sl, :] * c
                    hs[gi] = hg
                    c = jnp.broadcast_to(hg[last:last + 1, :], (SUBLANES, bw))
                new_carry.append(c)
                if with_out:
                    hv = jnp.concatenate(hs, axis=0)
                    if fwd:
                        acc_scr[rows, cols] = hv
                    else:
                        gate = g_ref[0, rows, cols].astype(F32)
                        y_ref[0, rows, cols] = ((acc_scr[rows, cols] + hv) * gate).astype(y_ref.dtype)
            return tuple(new_carry)

        init = tuple(jnp.broadcast_to(h0[:, k * bw:(k + 1) * bw], (SUBLANES, bw)) for k in range(LRU_BLOCKS))
        fin = lax.fori_loop(0, nchunk, chunk, init)
        if not with_out:
            for k in range(LRU_BLOCKS):
                (sf_ref if fwd else sb_ref)[0, :, k * bw:(k + 1) * bw] = fin[k][0:1, :]


def _lru(u_src, u_col, g_src, g_col, w_blk, gb_blk, lam, h0f, h0b, *, with_out):
    bsz, t, _ = u_src.shape
    w = LRU_WIDTH
    in_specs = [pl.BlockSpec((1, t, w), lambda b: (b, 0, u_col))]
    args = [u_src]
    if with_out:
        in_specs.append(pl.BlockSpec((1, t, w), lambda b: (b, 0, g_col)))
        args.append(g_src)
    in_specs += [pl.BlockSpec((LRU_BLOCKS, LRU_BLOCK_W, 4 * LRU_BLOCK_W), lambda b: (0, 0, 0)),
                 pl.BlockSpec((LRU_BLOCKS, 4 * LRU_BLOCK_W), lambda b: (0, 0)),
                 pl.BlockSpec((2, w), lambda b: (0, 0)),
                 pl.BlockSpec((1, 1, w), lambda b: (b, 0, 0)),
                 pl.BlockSpec((1, 1, w), lambda b: (b, 0, 0))]
    args += [w_blk, gb_blk, lam, h0f, h0b]
    if with_out:
        out_specs = pl.BlockSpec((1, t, w), lambda b: (b, 0, 0))
        out_shape = jax.ShapeDtypeStruct((bsz, t, w), BF16)
        scratch = [pltpu.VMEM((t, w), F32)]
    else:
        out_specs = [pl.BlockSpec((1, 1, w), lambda b: (b, 0, 0)),
                     pl.BlockSpec((1, 1, w), lambda b: (b, 0, 0))]
        out_shape = [jax.ShapeDtypeStruct((bsz, 1, w), F32),
                     jax.ShapeDtypeStruct((bsz, 1, w), F32)]
        scratch = []
    kern = functools.partial(_lru_kernel, t=t, with_out=with_out)
    return pl.pallas_call(
        kern,
        grid=(bsz,),
        in_specs=in_specs,
        out_specs=out_specs,
        out_shape=out_shape,
        scratch_shapes=scratch,
        compiler_params=pltpu.CompilerParams(
            dimension_semantics=("parallel",),
            vmem_limit_bytes=VMEM_LIMIT),
        name="lru_out" if with_out else "lru_state",
    )(*args)


def _post_kernel(x_ref, y_ref, z_ref, gate_ref, ylru_ref, mod_ref, nw_ref, wbs_ref, wbl_ref, wout_ref,
                 ln1g_ref, ln1b_ref, w1_ref, b1_ref, w2_ref, b2_ref, ln2g_ref, ln2b_ref, o_ref):
    d = D_MODEL
    x = x_ref[0]
    u = y_ref[0].astype(F32) * z_ref[0].astype(F32)
    parts = []
    for g in range(SSD_GROUPS):
        ug = u[:, g * SSD_GROUP_W:(g + 1) * SSD_GROUP_W]
        ms = jnp.mean(ug * ug, axis=-1, keepdims=True)
        parts.append(ug * lax.rsqrt(ms + RMS_EPS))
    un = (jnp.concatenate(parts, axis=1) * nw_ref[...]).astype(BF16)
    br_ssd = jnp.dot(un, wbs_ref[...], preferred_element_type=F32)
    br_lru = jnp.dot(ylru_ref[0], wbl_ref[...], preferred_element_type=F32)
    gates = gate_ref[0].astype(F32)
    merged = (gates[:, :d] * br_ssd + gates[:, d:] * br_lru).astype(BF16)
    x_mix = jnp.dot(merged, wout_ref[...], preferred_element_type=F32)
    mod = mod_ref[0]
    gate1 = mod[:, 2 * d:3 * d]
    shift2 = mod[:, 3 * d:4 * d]
    scale2 = mod[:, 4 * d:5 * d]
    gate2 = mod[:, 5 * d:6 * d]
    x1 = _layer_norm(DEEPNORM_ALPHA * x + gate1 * x_mix) * ln1g_ref[...] + ln1b_ref[...]
    h2 = (_layer_norm(x1) * (1.0 + scale2) + shift2).astype(BF16)
    hid = jnp.dot(h2, w1_ref[...], preferred_element_type=F32) + b1_ref[...]
    hid = jnp.maximum(hid, 0.0)
    hid = (hid * hid).astype(BF16)
    mlp = jnp.dot(hid, w2_ref[...], preferred_element_type=F32) + b2_ref[...]
    o_ref[0] = _layer_norm(DEEPNORM_ALPHA * x1 + gate2 * mlp) * ln2g_ref[...] + ln2b_ref[...]


def _const_spec(shape):
    nd = len(shape)
    return pl.BlockSpec(shape, lambda b, i: (0,) * nd, pipeline_mode=pl.Buffered(1))


def _post(x, y_ssd, proj, y_lru, mod, norm_w, w_br_ssd, w_br_lru, w_out, ln1_g, ln1_b,
          w1, b1, w2, b2, ln2_g, ln2_b, *, tb, z_col, gate_col):
    bsz, t, d = x.shape
    di = SSD_D_INNER
    row = lambda v: v.reshape(1, -1)
    return pl.pallas_call(
        _post_kernel,
        grid=(bsz, t // tb),
        in_specs=[pl.BlockSpec((1, tb, d), lambda b, i: (b, i, 0)),
                  pl.BlockSpec((1, tb, di), lambda b, i: (b, i, 0)),
                  pl.BlockSpec((1, tb, di), lambda b, i: (b, i, z_col)),
                  pl.BlockSpec((1, tb, di), lambda b, i: (b, i, gate_col)),
                  pl.BlockSpec((1, tb, d), lambda b, i: (b, i, 0)),
                  pl.BlockSpec((1, 1, N_MOD * d), lambda b, i: (b, 0, 0)),
                  _const_spec((1, di)),
                  _const_spec((di, d)),
                  _const_spec((d, d)),
                  _const_spec((d, d)),
                  _const_spec((1, d)),
                  _const_spec((1, d)),
                  _const_spec((d, MLP_HIDDEN)),
                  _const_spec((1, MLP_HIDDEN)),
                  _const_spec((MLP_HIDDEN, d)),
                  _const_spec((1, d)),
                  _const_spec((1, d)),
                  _const_spec((1, d))],
        out_specs=pl.BlockSpec((1, tb, d), lambda b, i: (b, i, 0)),
        out_shape=jax.ShapeDtypeStruct((bsz, t, d), F32),
        compiler_params=pltpu.CompilerParams(
            dimension_semantics=("parallel", "parallel"),
            vmem_limit_bytes=VMEM_LIMIT),
        name="post",
    )(x, y_ssd, proj, proj, y_lru, mod, row(norm_w), w_br_ssd.astype(BF16), w_br_lru.astype(BF16),
      w_out.astype(BF16), row(ln1_g), row(ln1_b), w1.astype(BF16), row(b1), w2.astype(BF16), row(b2),
      row(ln2_g), row(ln2_b))


def _pad_rows(v, n):
    return jnp.pad(v, ((0, n - v.shape[0]), (0, 0)))


def kernel(x, c, ctx, c_ctx, w_mod, b_mod, w_in, b_gate, ssd_conv_w, ssd_conv_b, ssd_dt_bias, ssd_a_log,
           ssd_d, ssd_norm_w, lru_conv_w, lru_conv_b, lru_wa, lru_ba, lru_wi, lru_bi, lru_lambda, w_br_ssd,
           w_br_lru, w_out, ln1_g, ln1_b, w_mlp1, b_mlp1, w_mlp2, b_mlp2, ln2_g, ln2_b):
    bsz, t, d = x.shape
    tctx = ctx.shape[1]
    l = 0
    wi = w_in[l]

    nrow = -(-(bsz + 1) // SUBLANES) * SUBLANES
    cvecs = _pad_rows(jnp.concatenate([c, c_ctx[None, :]], axis=0), nrow)
    mods = _modulation(cvecs, w_mod[l], b_mod[l])
    mod_x = mods[:bsz].reshape(bsz, 1, N_MOD * d)
    shift_x, scale_x = mod_x[:, :, :d], mod_x[:, :, d:2 * d]
    shift_c = jnp.broadcast_to(mods[bsz:bsz + 1, :d].reshape(1, 1, d), (bsz, 1, d))
    scale_c = jnp.broadcast_to(mods[bsz:bsz + 1, d:2 * d].reshape(1, 1, d), (bsz, 1, d))

    zc = lambda n: jnp.zeros((CONV_K, n), F32)
    z1 = lambda n: jnp.zeros((n,), F32)
    w_x = jnp.concatenate([wi[:, :SSD_XB], wi[:, O_C:O_Z], wi[:, O_Z:O_LRU_GATE], wi[:, O_MERGE:],
                           wi[:, O_LRU:STATE_COLS], wi[:, O_LRU_GATE:O_MERGE]], axis=1).astype(BF16)
    cw_x = _pad_rows(jnp.concatenate([ssd_conv_w[l], zc(SSD_D_INNER), zc(2 * d), lru_conv_w[l], zc(LRU_WIDTH)],
                                     axis=1), SUBLANES)
    cb_x = jnp.concatenate([ssd_conv_b[l], z1(SSD_D_INNER), b_gate[l], lru_conv_b[l], z1(LRU_WIDTH)])[None, :]
    n_xbc, n_z, n_m, n_l = SSD_XBC // PROJ_TN, SSD_D_INNER // PROJ_TN, 2 * d // PROJ_TN, LRU_WIDTH // PROJ_TN
    segs_x, lo = [], 0
    for n, kind in ((n_xbc, "conv_silu"), (n_z, "silu"), (n_m, "sigmoid_bias"), (n_l, "conv"), (n_l, "gelu")):
        segs_x.append((lo, lo + n, kind))
        lo += n
    z_col = SSD_XBC // SSD_D_INNER
    gate_col = z_col + 1
    lru_col = (SSD_XBC + 2 * SSD_D_INNER) // LRU_WIDTH
    lrug_col = lru_col + 1

    w_c = jnp.concatenate([wi[:, :SSD_XB], wi[:, O_LRU:STATE_COLS]], axis=1).astype(BF16)
    cw_c = _pad_rows(jnp.concatenate([ssd_conv_w[l][:, :SSD_XB], lru_conv_w[l]], axis=1), SUBLANES)
    cb_c = jnp.concatenate([ssd_conv_b[l][:SSD_XB], lru_conv_b[l]])[None, :]
    segs_c = [(0, SSD_XB // PROJ_TN, "conv_silu"), (SSD_XB // PROJ_TN, SSD_XB // PROJ_TN + n_l, "conv")]
    lru_col_c = SSD_XB // LRU_WIDTH

    w_dt = jnp.pad(wi[:, O_DT:O_LRU], ((0, 0), (0, DT_W - SSD_DT)))
    w_dt_hi = w_dt.astype(BF16)
    w_dt_lo = (w_dt - w_dt_hi.astype(F32)).astype(BF16)
    w_dt2 = jnp.concatenate([w_dt_hi, w_dt_lo], axis=1)
    dt_bias = jnp.pad(ssd_dt_bias[l].reshape(1, SSD_DT), ((0, 0), (0, DT_W - SSD_DT)))
    alog_row = jnp.pad(ssd_a_log[l].reshape(1, SSD_DT), ((0, 0), (0, DT_W - SSD_DT)))
    dskip = jnp.repeat(ssd_d[l], SSD_HEADDIM)[None, :]
    eye = jnp.eye(DT_W, dtype=BF16)
    hexp_f = jnp.repeat(eye[:, :SSD_HEADS], SSD_HEADDIM, axis=1)
    hexp_b = jnp.repeat(eye[:, SSD_HEADS:2 * SSD_HEADS], SSD_HEADDIM, axis=1)

    w_blk = jnp.concatenate([lru_wa[l, 0], lru_wi[l, 0], lru_wa[l, 1], lru_wi[l, 1]], axis=-1).astype(BF16)
    blk = lambda v: v.reshape(LRU_BLOCKS, LRU_BLOCK_W)
    gb_blk = jnp.concatenate([blk(lru_ba[l, 0]), blk(lru_bi[l, 0]), blk(lru_ba[l, 1]), blk(lru_bi[l, 1])], axis=-1)
    lam = lru_lambda[l]

    proj_c, dt_c = _inproj(ctx, shift_c, scale_c, w_c, cw_c, cb_c, w_dt2, dt_bias,
                           segs=segs_c, row_len=tctx, tb=tctx)
    zero_s = jnp.zeros((bsz, SSD_STATE, SSD_D_INNER), F32)
    zero_l = jnp.zeros((bsz, 1, LRU_WIDTH), F32)
    s_f = _ssd_sweep(proj_c, dt_c, alog_row, zero_s, hexp_f, direction="fwd", mode="state", xbc_width=SSD_XB)
    s_b = _ssd_sweep(proj_c, dt_c, alog_row, zero_s, hexp_b, direction="bwd", mode="state", xbc_width=SSD_XB)
    l_f, l_b = _lru(proj_c, lru_col_c, None, None, w_blk, gb_blk, lam, zero_l, zero_l, with_out=False)

    proj_x, dt_x = _inproj(x, shift_x, scale_x, w_x, cw_x, cb_x, w_dt2, dt_bias,
                           segs=segs_x, row_len=GRID_W, tb=min(t, PROJ_TB))
    y_b = _ssd_sweep(proj_x, dt_x, alog_row, s_b, hexp_b, direction="bwd", mode="inter", xbc_width=SSD_XBC)
    y_ssd = _ssd_sweep(proj_x, dt_x, alog_row, s_f, hexp_f, direction="fwd", mode="full", xbc_width=SSD_XBC,
                       dskip=dskip, yin=y_b)
    y_lru = _lru(proj_x, lru_col, proj_x, lrug_col, w_blk, gb_blk, lam, l_f, l_b, with_out=True)
    return _post(x, y_ssd, proj_x, y_lru, mod_x, ssd_norm_w[l], w_br_ssd[l], w_br_lru[l], w_out[l],
                 ln1_g[l], ln1_b[l], w_mlp1[l], b_mlp1[l], w_mlp2[l], b_mlp2[l], ln2_g[l], ln2_b[l],
                 tb=min(t, POST_TB), z_col=z_col, gate_col=gate_col)
```

```python
import functools

import jax
import jax.numpy as jnp
from jax import lax
from jax.experimental import pallas as pl
from jax.experimental.pallas import tpu as pltpu

F32 = jnp.float32
BF16 = jnp.bfloat16
HIGHEST = lax.Precision.HIGHEST

D_MODEL = 1024
GRID_W = 64
SSD_D_INNER = 2 * D_MODEL
SSD_HEADDIM = 64
SSD_HEADS = SSD_D_INNER // SSD_HEADDIM
SSD_GROUPS = 8
SSD_HPG = SSD_HEADS // SSD_GROUPS
SSD_STATE = 128
SSD_CHUNK = 128
SSD_GROUP_W = SSD_HPG * SSD_HEADDIM
CONV_K = 4
LRU_WIDTH = D_MODEL
LRU_BLOCKS = 8
LRU_BLOCK_W = LRU_WIDTH // LRU_BLOCKS
LRU_C = 8.0
MLP_HIDDEN = 4 * D_MODEL
N_MOD = 6
DEPTH = 1
DEEPNORM_ALPHA = (2 * DEPTH) ** 0.25
LN_EPS = 1e-6
RMS_EPS = 1e-5
LOG2E = 1.4426950408889634
assert SSD_CHUNK // 2 == SSD_HEADDIM

SSD_BC_W = SSD_GROUPS * SSD_STATE
SSD_XB = SSD_D_INNER + SSD_BC_W
SSD_XBC = SSD_D_INNER + 2 * SSD_BC_W
SSD_DT = 2 * SSD_HEADS
O_DT = SSD_XB
O_LRU = O_DT + SSD_DT
STATE_COLS = O_LRU + LRU_WIDTH
O_C = STATE_COLS
O_Z = O_C + SSD_BC_W
O_LRU_GATE = O_Z + SSD_D_INNER
O_MERGE = O_LRU_GATE + LRU_WIDTH

LANES = 128
SUBLANES = 8
VMEM_LIMIT = 56 * 1024 * 1024

PROJ_TN = 512
PROJ_SUB = 256
PROJ_TB = 512
CONV_GAP = SUBLANES
DT_W = LANES
SSD_TB = 512
LRU_TC = 128
POST_TB = 256


def _silu(v):
    return v / (1.0 + jnp.exp(-v))


def _sigmoid(v):
    return 1.0 / (1.0 + jnp.exp(-v))


def _softplus(v):
    return jnp.maximum(v, 0.0) + jnp.log1p(jnp.exp(-jnp.abs(v)))


def _layer_norm(v):
    mu = jnp.mean(v, axis=-1, keepdims=True)
    vc = v - mu
    var = jnp.mean(vc * vc, axis=-1, keepdims=True)
    return vc * lax.rsqrt(var + LN_EPS)


def _group_roll(v, shift):
    rows, width = v.shape
    v3 = v.reshape(rows // SUBLANES, SUBLANES, width)
    return pltpu.roll(v3, shift, 1).reshape(rows, width)


def _split_bf16(v):
    hi = v.astype(BF16)
    lo = (v - hi.astype(F32)).astype(BF16)
    return hi, lo


def _mod_kernel(c_ref, w_ref, b_ref, o_ref):
    s = _silu(c_ref[...])
    o_ref[...] = jnp.dot(s, w_ref[...], precision=HIGHEST, preferred_element_type=F32) + b_ref[...]


def _modulation(cvecs, w_mod, b_mod):
    rows, ncols = cvecs.shape[0], w_mod.shape[1]
    tn = 512
    return pl.pallas_call(
        _mod_kernel,
        grid=(ncols // tn,),
        in_specs=[pl.BlockSpec((rows, D_MODEL), lambda j: (0, 0)),
                  pl.BlockSpec((D_MODEL, tn), lambda j: (0, j)),
                  pl.BlockSpec((1, tn), lambda j: (0, j))],
        out_specs=pl.BlockSpec((rows, tn), lambda j: (0, j)),
        out_shape=jax.ShapeDtypeStruct((rows, ncols), F32),
        name="modulation",
    )(cvecs, w_mod, b_mod.reshape(1, ncols))


def _inproj_kernel(x_ref, shift_ref, scale_ref, w_ref, cw_ref, cb_ref, wdt_ref, dtb_ref,
                   p_ref, dt_ref, h_scr, c_scr, *, segs, row_len, tb):
    sub = PROJ_SUB
    npiece = sub // row_len
    pitch = row_len + CONV_GAP
    zgap = jnp.zeros((CONV_GAP, PROJ_TN), F32)
    for r in range(npiece + 1):
        c_scr[r * pitch:r * pitch + CONV_GAP, :] = zgap

    def sub_block(s, carry):
        rows = pl.ds(pl.multiple_of(s * sub, sub), sub)
        h = _layer_norm(x_ref[0, rows, :]) * (1.0 + scale_ref[0]) + shift_ref[0]
        hb = h.astype(BF16)
        h_scr[...] = hb
        dt2 = jnp.dot(hb, wdt_ref[...], preferred_element_type=F32)
        dt_ref[0, rows, :] = _softplus(dt2[:, :DT_W] + dt2[:, DT_W:] + dtb_ref[...])
        for lo, hi, kind in segs:
            for j in range(lo, hi):
                cols = slice(j * PROJ_TN, (j + 1) * PROJ_TN)
                acc = jnp.dot(h_scr[...], w_ref[:, cols], preferred_element_type=F32)
                cb = cb_ref[:, cols]
                if kind in ("conv_silu", "conv"):
                    cw = cw_ref[:, cols]
                    for r in range(npiece):
                        base = CONV_GAP + r * pitch
                        c_scr[base:base + row_len, :] = acc[r * row_len:(r + 1) * row_len, :]
                    pieces = []
                    for r in range(npiece):
                        base = CONV_GAP + r * pitch
                        o = cb + acc[r * row_len:(r + 1) * row_len, :] * cw[2:3]
                        o = o + c_scr[base - 2:base - 2 + row_len, :] * cw[0:1]
                        o = o + c_scr[base - 1:base - 1 + row_len, :] * cw[1:2]
                        o = o + c_scr[base + 1:base + 1 + row_len, :] * cw[3:4]
                        pieces.append(o)
                    out = pieces[0] if npiece == 1 else jnp.concatenate(pieces, axis=0)
                    if kind == "conv_silu":
                        out = _silu(out)
                elif kind == "silu":
                    out = _silu(acc)
                elif kind == "gelu":
                    out = 0.5 * acc * (1.0 + jnp.tanh(0.7978845608028654 * (acc + 0.044715 * acc * acc * acc)))
                elif kind == "sigmoid_bias":
                    out = _sigmoid(acc + cb)
                else:
                    raise ValueError(kind)
                p_ref[0, rows, cols] = out.astype(BF16)
        return carry

    lax.fori_loop(0, tb // sub, sub_block, 0)


def _inproj(xs, shift, scale, w_all, cw_all, cb_all, w_dt2, dt_bias, *, segs, row_len, tb):
    bsz, t, _ = xs.shape
    ncols = w_all.shape[1]
    kern = functools.partial(_inproj_kernel, segs=segs, row_len=row_len, tb=tb)
    const = lambda shape: pl.BlockSpec(shape, lambda b, i: (0, 0), pipeline_mode=pl.Buffered(1))
    conv_rows = CONV_GAP + (PROJ_SUB // row_len) * (row_len + CONV_GAP)
    return pl.pallas_call(
        kern,
        grid=(bsz, t // tb),
        in_specs=[pl.BlockSpec((1, tb, D_MODEL), lambda b, i: (b, i, 0)),
                  pl.BlockSpec((1, 1, D_MODEL), lambda b, i: (b, 0, 0)),
                  pl.BlockSpec((1, 1, D_MODEL), lambda b, i: (b, 0, 0)),
                  const((D_MODEL, ncols)),
                  const((SUBLANES, ncols)),
                  const((1, ncols)),
                  const((D_MODEL, 2 * DT_W)),
                  const((1, DT_W))],
        out_specs=[pl.BlockSpec((1, tb, ncols), lambda b, i: (b, i, 0)),
                   pl.BlockSpec((1, tb, DT_W), lambda b, i: (b, i, 0))],
        out_shape=[jax.ShapeDtypeStruct((bsz, t, ncols), BF16),
                   jax.ShapeDtypeStruct((bsz, t, DT_W), F32)],
        scratch_shapes=[pltpu.VMEM((PROJ_SUB, D_MODEL), BF16),
                        pltpu.VMEM((conv_rows, PROJ_TN), F32)],
        compiler_params=pltpu.CompilerParams(
            dimension_semantics=("parallel", "parallel"),
            vmem_limit_bytes=VMEM_LIMIT),
        name="inproj",
    )(xs, shift, scale, w_all, cw_all, cb_all, w_dt2, dt_bias)


def _ssd_kernel(*refs, direction, mode, nsteps, nsub):
    if mode == "full":
        (xbc_ref, dt_ref, alog_ref, h0_ref, hexp_ref, u_ref, lw_ref, lgb_ref, lam_ref, lh0_ref,
         dskip_ref, yin_ref, hin_ref, g_ref,
         y_ref, hout_ref, s_scr, e_scr, lc_scr, y_scr, cb_scr) = refs
    elif mode == "inter":
        (xbc_ref, dt_ref, alog_ref, h0_ref, hexp_ref, u_ref, lw_ref, lgb_ref, lam_ref, lh0_ref,
         y_ref, hout_ref, s_scr, e_scr, lc_scr) = refs
    else:
        xbc_ref, dt_ref, alog_ref, h0_ref, hexp_ref, sfin_ref, s_scr, e_scr = refs
    step = pl.program_id(1)
    L = SSD_CHUNK
    fwd = direction == "fwd"
    with_lru = mode != "state"
    groups = [slice(g * SSD_GROUP_W, (g + 1) * SSD_GROUP_W) for g in range(SSD_GROUPS)]
    bcols = [slice(SSD_D_INNER + g * SSD_STATE, SSD_D_INNER + (g + 1) * SSD_STATE) for g in range(SSD_GROUPS)]
    ccols = [slice(SSD_XB + g * SSD_STATE, SSD_XB + (g + 1) * SSD_STATE) for g in range(SSD_GROUPS)]

    @pl.when(step == 0)
    def _():
        s_scr[...] = h0_ref[0]
        if with_lru:
            lc_scr[...] = jnp.broadcast_to(lh0_ref[0], (SUBLANES, LRU_WIDTH))

    if with_lru:
        ldir = 0 if fwd else 1
        chalf, oks = _lru_consts(lam_ref[ldir:ldir + 1, :], fwd)
        ldcols = slice(ldir * 2 * LRU_BLOCK_W, (ldir + 1) * 2 * LRU_BLOCK_W)

    def lru_chunk(rows):
        for k in range(LRU_BLOCKS):
            cols = slice(k * LRU_BLOCK_W, (k + 1) * LRU_BLOCK_W)
            hv, c = _lru_block(u_ref[0, rows, cols], lw_ref[k, :, ldcols], lgb_ref[k:k + 1, ldcols],
                               chalf[:, cols], oks, lc_scr[:, cols], fwd)
            lc_scr[:, cols] = c
            if mode == "full":
                hv = (hv + hin_ref[0, rows, cols].astype(F32)) * g_ref[0, rows, cols].astype(F32)
            hout_ref[0, rows, cols] = hv.astype(hout_ref.dtype)

    a_neg = -jnp.exp(alog_ref[...]) * LOG2E
    ri = lax.broadcasted_iota(jnp.int32, (L, L), 0)
    ci = lax.broadcasted_iota(jnp.int32, (L, L), 1)
    tril = jnp.where(ri >= ci, 1.0, 0.0).astype(F32)
    triu = jnp.where(ri <= ci, 1.0, 0.0).astype(F32)

    def expand(vals, col0):
        hi, lo = _split_bf16(vals[:, col0:col0 + SSD_HEADS])
        return jnp.dot(jnp.concatenate([hi, lo], axis=1), hexp_ref[...], preferred_element_type=F32)

    def chunk(i, carry):
        cidx = i if fwd else nsub - 1 - i
        rows = pl.ds(pl.multiple_of(cidx * L, L), L)
        if with_lru:
            lru_chunk(rows)
        dt = dt_ref[0, rows, :]
        dta = dt * a_neg
        cum = jnp.dot(tril if fwd else triu, dta, precision=HIGHEST, preferred_element_type=F32)
        total = cum[L - 1:L, :] if fwd else cum[0:1, :]
        w_state = dt * jnp.exp2(total - cum)
        dec_tot = jnp.broadcast_to(jnp.exp2(total), (SUBLANES, DT_W))
        if mode == "state":
            stack = jnp.concatenate([w_state, dec_tot], axis=0)
        elif mode == "inter":
            stack = jnp.concatenate([w_state, dec_tot, jnp.exp2(cum)], axis=0)
        else:
            stack = jnp.concatenate([w_state, dec_tot, cum], axis=0)
        e_scr[...] = expand(stack, 0 if fwd else SSD_HEADS)
        r_dec = L
        r_ein = L + SUBLANES

        if mode != "state":
            for g, gx in enumerate(groups):
                cg = xbc_ref[0, rows, ccols[g]]
                e_in = e_scr[r_ein:r_ein + L, gx]
                if mode == "full":
                    e_in = jnp.exp2(e_in)
                y_g = e_in * jnp.dot(cg, s_scr[:, gx].astype(BF16), preferred_element_type=F32)
                if mode == "inter":
                    y_ref[0, rows, gx] = y_g.astype(y_ref.dtype)
                else:
                    xg = xbc_ref[0, rows, gx].astype(F32)
                    y_scr[:, gx] = y_g + yin_ref[0, rows, gx].astype(F32) + dskip_ref[:, gx] * xg

        for g, gx in enumerate(groups):
            bg = xbc_ref[0, rows, bcols[g]]
            xw = (xbc_ref[0, rows, gx].astype(F32) * e_scr[0:L, gx]).astype(BF16)
            upd = lax.dot_general(bg, xw, (((0,), (0,)), ((), ())), preferred_element_type=F32)
            s_scr[:, gx] = e_scr[r_dec:r_dec + 1, gx] * s_scr[:, gx] + upd

        if mode == "full":
            P = SSD_HEADDIM
            half = L // 2
            cum_b = jnp.dot(triu, dta, precision=HIGHEST, preferred_element_type=F32)
            cb_scr[...] = expand(cum_b, SSD_HEADS)
            dta_t = dta.T
            ldt_t = jnp.log2(dt.T)
            row_f = ldt_t - jnp.dot(dta_t, triu, precision=HIGHEST, preferred_element_type=F32)
            row_b = ldt_t - jnp.dot(dta_t, tril, precision=HIGHEST, preferred_element_type=F32)
            ti = lax.broadcasted_iota(jnp.int32, (half, 2 * P), 0)
            tj = lax.broadcasted_iota(jnp.int32, (half, 2 * P), 1) % P
            lo_tri = ti >= tj
            up_tri = ti <= tj
            lane_a = lax.broadcasted_iota(jnp.int32, (half, 2 * P), 1) < P
            for g in range(SSD_GROUPS):
                bg = xbc_ref[0, rows, bcols[g]]
                cg = xbc_ref[0, rows, ccols[g]]
                b4 = jnp.concatenate([bg[:half, :], bg[:half, :], bg[half:, :], bg[half:, :]], axis=0)
                cb4 = lax.dot_general(cg, b4, (((1,), (1,)), ((), ())), preferred_element_type=F32)
                for pp in range(SSD_HPG // 2):
                    ha = g * SSD_HPG + 2 * pp
                    hb = SSD_HEADS + ha
                    px = slice(ha * P, (ha + 2) * P)
                    col_f = e_scr[r_ein:r_ein + L, px]
                    col_b = cb_scr[:, px]
                    xp = xbc_ref[0, rows, px]
                    ms, rhss = [], []
                    for jh in range(2):
                        js = slice(jh * half, (jh + 1) * half)
                        rf = jnp.concatenate([row_f[ha:ha + 1, js], row_f[ha + 1:ha + 2, js]], axis=1)
                        rb = jnp.concatenate([row_b[hb:hb + 1, js], row_b[hb + 1:hb + 2, js]], axis=1)
                        dg = slice(jh * half, (jh + 1) * half)
                        m_diag = (jnp.where(lo_tri, jnp.exp2(col_f[dg, :] + rf), 0.0)
                                  + jnp.where(up_tri, jnp.exp2(col_b[dg, :] + rb), 0.0))
                        if jh == 0:
                            m = jnp.concatenate([m_diag, jnp.exp2(col_f[half:, :] + rf)], axis=0)
                        else:
                            m = jnp.concatenate([jnp.exp2(col_b[:half, :] + rb), m_diag], axis=0)
                        ms.append((m * cb4[:, jh * 2 * P:(jh + 1) * 2 * P]).astype(BF16))
                        xj = xp[js, :]
                        zero = jnp.zeros_like(xj)
                        rhss += [jnp.where(lane_a, xj, zero), jnp.where(lane_a, zero, xj)]
                    y2 = jnp.dot(jnp.concatenate(ms, axis=1), jnp.concatenate(rhss, axis=0),
                                 preferred_element_type=F32)
                    y_ref[0, rows, px] = (y_scr[:, px] + y2).astype(y_ref.dtype)
        return carry

    lax.fori_loop(0, nsub, chunk, 0)

    if mode == "state":
        @pl.when(step == nsteps - 1)
        def _():
            sfin_ref[0] = s_scr[...]


def _ssd_sweep(xbc, dt, alog_row, h0, hexp, *, direction, mode, xbc_width, lru=None, dskip=None, yin=None,
               hin=None):
    bsz, t, _ = xbc.shape
    tb = min(t, SSD_TB)
    nsteps = t // tb
    hp = SSD_D_INNER
    w = LRU_WIDTH
    row = (lambda c: c) if direction == "fwd" else (lambda c: nsteps - 1 - c)
    cmap = lambda b, c: (b, row(c), 0)
    const = lambda shape: pl.BlockSpec(shape, lambda b, c: (0,) * len(shape))
    in_specs = [pl.BlockSpec((1, tb, xbc_width), cmap),
                pl.BlockSpec((1, tb, DT_W), cmap),
                const((1, DT_W)),
                pl.BlockSpec((1, SSD_STATE, hp), lambda b, c: (b, 0, 0)),
                const((2 * SSD_HEADS, hp))]
    args = [xbc, dt, alog_row, h0, jnp.concatenate([hexp, hexp], axis=0)]
    stack_rows = (SSD_CHUNK + SUBLANES) if mode == "state" else (2 * SSD_CHUNK + SUBLANES)
    scratch = [pltpu.VMEM((SSD_STATE, hp), F32), pltpu.VMEM((stack_rows, hp), F32)]
    if mode != "state":
        proj, u_col, g_col, w_blk, gb_blk, lam, lh0 = lru
        in_specs += [pl.BlockSpec((1, tb, w), lambda b, c: (b, row(c), u_col)),
                     const((LRU_BLOCKS, LRU_BLOCK_W, 4 * LRU_BLOCK_W)),
                     const((LRU_BLOCKS, 4 * LRU_BLOCK_W)),
                     const((2, w)),
                     pl.BlockSpec((1, 1, w), lambda b, c: (b, 0, 0))]
        args += [proj, w_blk, gb_blk, lam, lh0]
        scratch.append(pltpu.VMEM((SUBLANES, w), F32))
    if mode == "full":
        scratch += [pltpu.VMEM((SSD_CHUNK, hp), F32), pltpu.VMEM((SSD_CHUNK, hp), F32)]
        in_specs += [const((1, hp)), pl.BlockSpec((1, tb, hp), cmap), pl.BlockSpec((1, tb, w), cmap),
                     pl.BlockSpec((1, tb, w), lambda b, c: (b, row(c), g_col))]
        args += [dskip, yin, hin, proj]
    if mode == "state":
        out_specs = pl.BlockSpec((1, SSD_STATE, hp), lambda b, c: (b, 0, 0))
        out_shape = jax.ShapeDtypeStruct((bsz, SSD_STATE, hp), F32)
    else:
        out_specs = [pl.BlockSpec((1, tb, hp), cmap), pl.BlockSpec((1, tb, w), cmap)]
        out_shape = [jax.ShapeDtypeStruct((bsz, t, hp), BF16), jax.ShapeDtypeStruct((bsz, t, w), BF16)]
    kern = functools.partial(_ssd_kernel, direction=direction, mode=mode, nsteps=nsteps, nsub=tb // SSD_CHUNK)
    return pl.pallas_call(
        kern,
        grid=(bsz, nsteps),
        in_specs=in_specs,
        out_specs=out_specs,
        out_shape=out_shape,
        scratch_shapes=scratch,
        compiler_params=pltpu.CompilerParams(
            dimension_semantics=("parallel", "arbitrary"),
            vmem_limit_bytes=VMEM_LIMIT),
        name=f"ssd_{direction}_{mode}",
    )(*args)


def _lru_consts(lam_row, fwd):
    chalf = (-0.5 * LRU_C) * _softplus(-lam_row)
    sub_idx = lax.broadcasted_iota(jnp.int32, (LRU_TC, LRU_BLOCK_W), 0) % SUBLANES
    oks = {s: (sub_idx >= s) if fwd else (sub_idx < SUBLANES - s) for s in (1, 2, 4)}
    return chalf, oks


def _lru_block(ub, wk, gbk, chalf_k, oks, c, fwd):
    bw = LRU_BLOCK_W
    ngrp = LRU_TC // SUBLANES
    pre = jnp.dot(ub, wk, preferred_element_type=F32) + gbk
    tr = jnp.tanh(pre[:, :bw])
    ti = jnp.tanh(pre[:, bw:])
    log_a = tr * chalf_k + chalf_k
    a = jnp.exp(log_a)
    th = jnp.tanh(log_a)
    b = jnp.sqrt(-0.5 * th / (1.0 - th)) * ((ti + 1.0) * ub.astype(F32))
    for s in (1, 2, 4):
        ok = oks[s]
        shift = s if fwd else SUBLANES - s
        b = b + jnp.where(ok, a, 0.0) * _group_roll(b, shift)
        a = a * jnp.where(ok, _group_roll(a, shift), 1.0)
    hs = [None] * ngrp
    order = range(ngrp) if fwd else range(ngrp - 1, -1, -1)
    last = SUBLANES - 1 if fwd else 0
    for gi in order:
        sl = slice(gi * SUBLANES, (gi + 1) * SUBLANES)
        hg = b[sl, :] + a[sl, :] * c
        hs[gi] = hg
        c = jnp.broadcast_to(hg[last:last + 1, :], (SUBLANES, bw))
    return jnp.concatenate(hs, axis=0), c


def _lru_state_kernel(u_ref, w_ref, gb_ref, lam_ref, sf_ref, sb_ref, *, t):
    bw = LRU_BLOCK_W
    nchunk = t // LRU_TC
    for d in range(2):
        fwd = d == 0
        chalf, oks = _lru_consts(lam_ref[d:d + 1, :], fwd)
        dcols = slice(d * 2 * bw, (d + 1) * 2 * bw)

        def chunk(i, carry, fwd=fwd, chalf=chalf, oks=oks, dcols=dcols):
            cidx = i if fwd else nchunk - 1 - i
            rows = pl.ds(pl.multiple_of(cidx * LRU_TC, LRU_TC), LRU_TC)
            out = []
            for k in range(LRU_BLOCKS):
                cols = slice(k * bw, (k + 1) * bw)
                _, c = _lru_block(u_ref[0, rows, cols], w_ref[k, :, dcols], gb_ref[k:k + 1, dcols],
                                  chalf[:, cols], oks, carry[k], fwd)
                out.append(c)
            return tuple(out)

        init = tuple(jnp.zeros((SUBLANES, bw), F32) for _ in range(LRU_BLOCKS))
        fin = lax.fori_loop(0, nchunk, chunk, init)
        for k in range(LRU_BLOCKS):
            (sf_ref if fwd else sb_ref)[0, :, k * bw:(k + 1) * bw] = fin[k][0:1, :]


def _lru_state(u_src, u_col, w_blk, gb_blk, lam):
    bsz, t, _ = u_src.shape
    w = LRU_WIDTH
    return pl.pallas_call(
        functools.partial(_lru_state_kernel, t=t),
        grid=(bsz,),
        in_specs=[pl.BlockSpec((1, t, w), lambda b: (b, 0, u_col)),
                  pl.BlockSpec((LRU_BLOCKS, LRU_BLOCK_W, 4 * LRU_BLOCK_W), lambda b: (0, 0, 0)),
                  pl.BlockSpec((LRU_BLOCKS, 4 * LRU_BLOCK_W), lambda b: (0, 0)),
                  pl.BlockSpec((2, w), lambda b: (0, 0))],
        out_specs=[pl.BlockSpec((1, 1, w), lambda b: (b, 0, 0)),
                   pl.BlockSpec((1, 1, w), lambda b: (b, 0, 0))],
        out_shape=[jax.ShapeDtypeStruct((bsz, 1, w), F32),
                   jax.ShapeDtypeStruct((bsz, 1, w), F32)],
        compiler_params=pltpu.CompilerParams(
            dimension_semantics=("parallel",),
            vmem_limit_bytes=VMEM_LIMIT),
        name="lru_state",
    )(u_src, w_blk, gb_blk, lam)


def _post_kernel(x_ref, y_ref, z_ref, gate_ref, ylru_ref, mod_ref, nw_ref, wbs_ref, wbl_ref, wout_ref,
                 ln1g_ref, ln1b_ref, w1_ref, b1_ref, w2_ref, b2_ref, ln2g_ref, ln2b_ref, o_ref):
    d = D_MODEL
    x = x_ref[0]
    u = y_ref[0].astype(F32) * z_ref[0].astype(F32)
    parts = []
    for g in range(SSD_GROUPS):
        ug = u[:, g * SSD_GROUP_W:(g + 1) * SSD_GROUP_W]
        ms = jnp.mean(ug * ug, axis=-1, keepdims=True)
        parts.append(ug * lax.rsqrt(ms + RMS_EPS))
    un = (jnp.concatenate(parts, axis=1) * nw_ref[...]).astype(BF16)
    br_ssd = jnp.dot(un, wbs_ref[...], preferred_element_type=F32)
    br_lru = jnp.dot(ylru_ref[0], wbl_ref[...], preferred_element_type=F32)
    gates = gate_ref[0].astype(F32)
    merged = (gates[:, :d] * br_ssd + gates[:, d:] * br_lru).astype(BF16)
    x_mix = jnp.dot(merged, wout_ref[...], preferred_element_type=F32)
    mod = mod_ref[0]
    gate1 = mod[:, 2 * d:3 * d]
    shift2 = mod[:, 3 * d:4 * d]
    scale2 = mod[:, 4 * d:5 * d]
    gate2 = mod[:, 5 * d:6 * d]
    x1 = _layer_norm(DEEPNORM_ALPHA * x + gate1 * x_mix) * ln1g_ref[...] + ln1b_ref[...]
    h2 = (_layer_norm(x1) * (1.0 + scale2) + shift2).astype(BF16)
    hid = jnp.dot(h2, w1_ref[...], preferred_element_type=F32) + b1_ref[...]
    hid = jnp.maximum(hid, 0.0)
    hid = (hid * hid).astype(BF16)
    mlp = jnp.dot(hid, w2_ref[...], preferred_element_type=F32) + b2_ref[...]
    o_ref[0] = _layer_norm(DEEPNORM_ALPHA * x1 + gate2 * mlp) * ln2g_ref[...] + ln2b_ref[...]


def _const_spec(shape):
    nd = len(shape)
    return pl.BlockSpec(shape, lambda b, i: (0,) * nd, pipeline_mode=pl.Buffered(1))


def _post(x, y_ssd, proj, y_lru, mod, norm_w, w_br_ssd, w_br_lru, w_out, ln1_g, ln1_b,
          w1, b1, w2, b2, ln2_g, ln2_b, *, tb, z_col, gate_col):
    bsz, t, d = x.shape
    di = SSD_D_INNER
    row = lambda v: v.reshape(1, -1)
    return pl.pallas_call(
        _post_kernel,
        grid=(bsz, t // tb),
        in_specs=[pl.BlockSpec((1, tb, d), lambda b, i: (b, i, 0)),
                  pl.BlockSpec((1, tb, di), lambda b, i: (b, i, 0)),
                  pl.BlockSpec((1, tb, di), lambda b, i: (b, i, z_col)),
                  pl.BlockSpec((1, tb, di), lambda b, i: (b, i, gate_col)),
                  pl.BlockSpec((1, tb, d), lambda b, i: (b, i, 0)),
                  pl.BlockSpec((1, 1, N_MOD * d), lambda b, i: (b, 0, 0)),
                  _const_spec((1, di)),
                  _const_spec((di, d)),
                  _const_spec((d, d)),
                  _const_spec((d, d)),
                  _const_spec((1, d)),
                  _const_spec((1, d)),
                  _const_spec((d, MLP_HIDDEN)),
                  _const_spec((1, MLP_HIDDEN)),
                  _const_spec((MLP_HIDDEN, d)),
                  _const_spec((1, d)),
                  _const_spec((1, d)),
                  _const_spec((1, d))],
        out_specs=pl.BlockSpec((1, tb, d), lambda b, i: (b, i, 0)),
        out_shape=jax.ShapeDtypeStruct((bsz, t, d), F32),
        compiler_params=pltpu.CompilerParams(
            dimension_semantics=("parallel", "parallel"),
            vmem_limit_bytes=VMEM_LIMIT),
        name="post",
    )(x, y_ssd, proj, proj, y_lru, mod, row(norm_w), w_br_ssd.astype(BF16), w_br_lru.astype(BF16),
      w_out.astype(BF16), row(ln1_g), row(ln1_b), w1.astype(BF16), row(b1), w2.astype(BF16), row(b2),
      row(ln2_g), row(ln2_b))


def _pad_rows(v, n):
    return jnp.pad(v, ((0, n - v.shape[0]), (0, 0)))


def kernel(x, c, ctx, c_ctx, w_mod, b_mod, w_in, b_gate, ssd_conv_w, ssd_conv_b, ssd_dt_bias, ssd_a_log,
           ssd_d, ssd_norm_w, lru_conv_w, lru_conv_b, lru_wa, lru_ba, lru_wi, lru_bi, lru_lambda, w_br_ssd,
           w_br_lru, w_out, ln1_g, ln1_b, w_mlp1, b_mlp1, w_mlp2, b_mlp2, ln2_g, ln2_b):
    bsz, t, d = x.shape
    tctx = ctx.shape[1]
    l = 0
    wi = w_in[l]

    nrow = -(-(bsz + 1) // SUBLANES) * SUBLANES
    cvecs = _pad_rows(jnp.concatenate([c, c_ctx[None, :]], axis=0), nrow)
    mods = _modulation(cvecs, w_mod[l], b_mod[l])
    mod_x = mods[:bsz].reshape(bsz, 1, N_MOD * d)
    shift_x, scale_x = mod_x[:, :, :d], mod_x[:, :, d:2 * d]
    shift_c = jnp.broadcast_to(mods[bsz:bsz + 1, :d].reshape(1, 1, d), (bsz, 1, d))
    scale_c = jnp.broadcast_to(mods[bsz:bsz + 1, d:2 * d].reshape(1, 1, d), (bsz, 1, d))

    zc = lambda n: jnp.zeros((CONV_K, n), F32)
    z1 = lambda n: jnp.zeros((n,), F32)
    w_x = jnp.concatenate([wi[:, :SSD_XB], wi[:, O_C:O_Z], wi[:, O_Z:O_LRU_GATE], wi[:, O_MERGE:],
                           wi[:, O_LRU:STATE_COLS], wi[:, O_LRU_GATE:O_MERGE]], axis=1).astype(BF16)
    cw_x = _pad_rows(jnp.concatenate([ssd_conv_w[l], zc(SSD_D_INNER), zc(2 * d), lru_conv_w[l], zc(LRU_WIDTH)],
                                     axis=1), SUBLANES)
    cb_x = jnp.concatenate([ssd_conv_b[l], z1(SSD_D_INNER), b_gate[l], lru_conv_b[l], z1(LRU_WIDTH)])[None, :]
    n_xbc, n_z, n_m, n_l = SSD_XBC // PROJ_TN, SSD_D_INNER // PROJ_TN, 2 * d // PROJ_TN, LRU_WIDTH // PROJ_TN
    segs_x, lo = [], 0
    for n, kind in ((n_xbc, "conv_silu"), (n_z, "silu"), (n_m, "sigmoid_bias"), (n_l, "conv"), (n_l, "gelu")):
        segs_x.append((lo, lo + n, kind))
        lo += n
    z_col = SSD_XBC // SSD_D_INNER
    gate_col = z_col + 1
    lru_col = (SSD_XBC + 2 * SSD_D_INNER) // LRU_WIDTH
    lrug_col = lru_col + 1

    w_c = jnp.concatenate([wi[:, :SSD_XB], wi[:, O_LRU:STATE_COLS]], axis=1).astype(BF16)
    cw_c = _pad_rows(jnp.concatenate([ssd_conv_w[l][:, :SSD_XB], lru_conv_w[l]], axis=1), SUBLANES)
    cb_c = jnp.concatenate([ssd_conv_b[l][:SSD_XB], lru_conv_b[l]])[None, :]
    segs_c = [(0, SSD_XB // PROJ_TN, "conv_silu"), (SSD_XB // PROJ_TN, SSD_XB // PROJ_TN + n_l, "conv")]
    lru_col_c = SSD_XB // LRU_WIDTH

    w_dt = jnp.pad(wi[:, O_DT:O_LRU], ((0, 0), (0, DT_W - SSD_DT)))
    w_dt_hi = w_dt.astype(BF16)
    w_dt_lo = (w_dt - w_dt_hi.astype(F32)).astype(BF16)
    w_dt2 = jnp.concatenate([w_dt_hi, w_dt_lo], axis=1)
    dt_bias = jnp.pad(ssd_dt_bias[l].reshape(1, SSD_DT), ((0, 0), (0, DT_W - SSD_DT)))
    alog_row = jnp.pad(ssd_a_log[l].reshape(1, SSD_DT), ((0, 0), (0, DT_W - SSD_DT)))
    dskip = jnp.repeat(ssd_d[l], SSD_HEADDIM)[None, :]
    hexp = jnp.repeat(jnp.eye(SSD_HEADS, dtype=BF16), SSD_HEADDIM, axis=1)

    w_blk = (0.5 * jnp.concatenate([lru_wa[l, 0], lru_wi[l, 0], lru_wa[l, 1], lru_wi[l, 1]], axis=-1)).astype(BF16)
    blk = lambda v: v.reshape(LRU_BLOCKS, LRU_BLOCK_W)
    gb_blk = 0.5 * jnp.concatenate([blk(lru_ba[l, 0]), blk(lru_bi[l, 0]), blk(lru_ba[l, 1]), blk(lru_bi[l, 1])],
                                   axis=-1)
    lam = lru_lambda[l]

    proj_c, dt_c = _inproj(ctx, shift_c, scale_c, w_c, cw_c, cb_c, w_dt2, dt_bias,
                           segs=segs_c, row_len=tctx, tb=tctx)
    zero_s = jnp.zeros((bsz, SSD_STATE, SSD_D_INNER), F32)
    s_f = _ssd_sweep(proj_c, dt_c, alog_row, zero_s, hexp, direction="fwd", mode="state", xbc_width=SSD_XB)
    s_b = _ssd_sweep(proj_c, dt_c, alog_row, zero_s, hexp, direction="bwd", mode="state", xbc_width=SSD_XB)
    l_f, l_b = _lru_state(proj_c, lru_col_c, w_blk, gb_blk, lam)

    proj_x, dt_x = _inproj(x, shift_x, scale_x, w_x, cw_x, cb_x, w_dt2, dt_bias,
                           segs=segs_x, row_len=GRID_W, tb=min(t, PROJ_TB))
    y_b, h_b = _ssd_sweep(proj_x, dt_x, alog_row, s_b, hexp, direction="bwd", mode="inter", xbc_width=SSD_XBC,
                          lru=(proj_x, lru_col, lrug_col, w_blk, gb_blk, lam, l_b))
    y_ssd, y_lru = _ssd_sweep(proj_x, dt_x, alog_row, s_f, hexp, direction="fwd", mode="full", xbc_width=SSD_XBC,
                              lru=(proj_x, lru_col, lrug_col, w_blk, gb_blk, lam, l_f),
                              dskip=dskip, yin=y_b, hin=h_b)
    return _post(x, y_ssd, proj_x, y_lru, mod_x, ssd_norm_w[l], w_br_ssd[l], w_br_lru[l], w_out[l],
                 ln1_g[l], ln1_b[l], w_mlp1[l], b_mlp1[l], w_mlp2[l], b_mlp2[l], ln2_g[l], ln2_b[l],
                 tb=min(t, POST_TB), z_col=z_col, gate_col=gate_col)
```

```python
import functools

import jax
import jax.numpy as jnp
from jax import lax
from jax.experimental import pallas as pl
from jax.experimental.pallas import tpu as pltpu

F32 = jnp.float32
BF16 = jnp.bfloat16
HIGHEST = lax.Precision.HIGHEST

D_MODEL = 1024
GRID_W = 64
SSD_D_INNER = 2 * D_MODEL
SSD_HEADDIM = 64
SSD_HEADS = SSD_D_INNER // SSD_HEADDIM
SSD_GROUPS = 8
SSD_HPG = SSD_HEADS // SSD_GROUPS
SSD_STATE = 128
SSD_CHUNK = 128
SSD_GROUP_W = SSD_HPG * SSD_HEADDIM
CONV_K = 4
LRU_WIDTH = D_MODEL
LRU_BLOCKS = 8
LRU_BLOCK_W = LRU_WIDTH // LRU_BLOCKS
LRU_C = 8.0
MLP_HIDDEN = 4 * D_MODEL
N_MOD = 6
DEPTH = 1
DEEPNORM_ALPHA = (2 * DEPTH) ** 0.25
LN_EPS = 1e-6
RMS_EPS = 1e-5
LOG2E = 1.4426950408889634
assert SSD_CHUNK // 2 == SSD_HEADDIM

SSD_BC_W = SSD_GROUPS * SSD_STATE
SSD_XB = SSD_D_INNER + SSD_BC_W
SSD_XBC = SSD_D_INNER + 2 * SSD_BC_W
SSD_DT = 2 * SSD_HEADS
O_DT = SSD_XB
O_LRU = O_DT + SSD_DT
STATE_COLS = O_LRU + LRU_WIDTH
O_C = STATE_COLS
O_Z = O_C + SSD_BC_W
O_LRU_GATE = O_Z + SSD_D_INNER
O_MERGE = O_LRU_GATE + LRU_WIDTH

LANES = 128
SUBLANES = 8
VMEM_LIMIT = 56 * 1024 * 1024

PROJ_TN = 512
PROJ_SUB = 256
PROJ_TB = 512
CONV_GAP = SUBLANES
DT_W = LANES
SSD_TB = 256
SSD_NB = 2
LRU_TC = 128
LRU_VT = 32
POST_TB = 256


def _silu(v):
    return v / (1.0 + jnp.exp(-v))


def _sigmoid(v):
    return 1.0 / (1.0 + jnp.exp(-v))


def _softplus(v):
    return jnp.maximum(v, 0.0) + jnp.log1p(jnp.exp(-jnp.abs(v)))


def _layer_norm(v):
    mu = jnp.mean(v, axis=-1, keepdims=True)
    vc = v - mu
    var = jnp.mean(vc * vc, axis=-1, keepdims=True)
    return vc * lax.rsqrt(var + LN_EPS)


def _group_roll(v, shift):
    rows, width = v.shape
    v3 = v.reshape(rows // SUBLANES, SUBLANES, width)
    return pltpu.roll(v3, shift, 1).reshape(rows, width)


def _split_bf16(v):
    hi = v.astype(BF16)
    lo = (v - hi.astype(F32)).astype(BF16)
    return hi, lo


def _mod_kernel(c_ref, w_ref, b_ref, o_ref):
    s = _silu(c_ref[...])
    o_ref[...] = jnp.dot(s, w_ref[...], precision=HIGHEST, preferred_element_type=F32) + b_ref[...]


def _modulation(cvecs, w_mod, b_mod):
    rows, ncols = cvecs.shape[0], w_mod.shape[1]
    tn = 512
    return pl.pallas_call(
        _mod_kernel,
        grid=(ncols // tn,),
        in_specs=[pl.BlockSpec((rows, D_MODEL), lambda j: (0, 0)),
                  pl.BlockSpec((D_MODEL, tn), lambda j: (0, j)),
                  pl.BlockSpec((1, tn), lambda j: (0, j))],
        out_specs=pl.BlockSpec((rows, tn), lambda j: (0, j)),
        out_shape=jax.ShapeDtypeStruct((rows, ncols), F32),
        name="modulation",
    )(cvecs, w_mod, b_mod.reshape(1, ncols))


def _inproj_kernel(x_ref, shift_ref, scale_ref, w_ref, cw_ref, cb_ref, wdt_ref, dtb_ref,
                   p_ref, dt_ref, h_scr, c_scr, *, segs, row_len, tb):
    sub = PROJ_SUB
    npiece = sub // row_len
    pitch = row_len + CONV_GAP
    zgap = jnp.zeros((CONV_GAP, PROJ_TN), F32)
    for r in range(npiece + 1):
        c_scr[r * pitch:r * pitch + CONV_GAP, :] = zgap

    def sub_block(s, carry):
        rows = pl.ds(pl.multiple_of(s * sub, sub), sub)
        h = _layer_norm(x_ref[0, rows, :]) * (1.0 + scale_ref[0]) + shift_ref[0]
        hb = h.astype(BF16)
        h_scr[...] = hb
        dt2 = jnp.dot(hb, wdt_ref[...], preferred_element_type=F32)
        dt_ref[0, rows, :] = _softplus(dt2[:, :DT_W] + dt2[:, DT_W:] + dtb_ref[...])
        for lo, hi, kind in segs:
            for j in range(lo, hi):
                cols = slice(j * PROJ_TN, (j + 1) * PROJ_TN)
                acc = jnp.dot(h_scr[...], w_ref[:, cols], preferred_element_type=F32)
                cb = cb_ref[:, cols]
                if kind in ("conv_silu", "conv"):
                    cw = cw_ref[:, cols]
                    for r in range(npiece):
                        base = CONV_GAP + r * pitch
                        c_scr[base:base + row_len, :] = acc[r * row_len:(r + 1) * row_len, :]
                    pieces = []
                    for r in range(npiece):
                        base = CONV_GAP + r * pitch
                        o = cb + acc[r * row_len:(r + 1) * row_len, :] * cw[2:3]
                        o = o + c_scr[base - 2:base - 2 + row_len, :] * cw[0:1]
                        o = o + c_scr[base - 1:base - 1 + row_len, :] * cw[1:2]
                        o = o + c_scr[base + 1:base + 1 + row_len, :] * cw[3:4]
                        pieces.append(o)
                    out = pieces[0] if npiece == 1 else jnp.concatenate(pieces, axis=0)
                    if kind == "conv_silu":
                        out = _silu(out)
                elif kind == "silu":
                    out = _silu(acc)
                elif kind == "gelu":
                    out = 0.5 * acc * (1.0 + jnp.tanh(0.7978845608028654 * (acc + 0.044715 * acc * acc * acc)))
                elif kind == "sigmoid_bias":
                    out = _sigmoid(acc + cb)
                else:
                    raise ValueError(kind)
                p_ref[0, rows, cols] = out.astype(BF16)
        return carry

    lax.fori_loop(0, tb // sub, sub_block, 0)


def _inproj(xs, shift, scale, w_all, cw_all, cb_all, w_dt2, dt_bias, *, segs, row_len, tb):
    bsz, t, _ = xs.shape
    ncols = w_all.shape[1]
    kern = functools.partial(_inproj_kernel, segs=segs, row_len=row_len, tb=tb)
    const = lambda shape: pl.BlockSpec(shape, lambda b, i: (0, 0), pipeline_mode=pl.Buffered(1))
    conv_rows = CONV_GAP + (PROJ_SUB // row_len) * (row_len + CONV_GAP)
    return pl.pallas_call(
        kern,
        grid=(bsz, t // tb),
        in_specs=[pl.BlockSpec((1, tb, D_MODEL), lambda b, i: (b, i, 0)),
                  pl.BlockSpec((1, 1, D_MODEL), lambda b, i: (b, 0, 0)),
                  pl.BlockSpec((1, 1, D_MODEL), lambda b, i: (b, 0, 0)),
                  const((D_MODEL, ncols)),
                  const((SUBLANES, ncols)),
                  const((1, ncols)),
                  const((D_MODEL, 2 * DT_W)),
                  const((1, DT_W))],
        out_specs=[pl.BlockSpec((1, tb, ncols), lambda b, i: (b, i, 0)),
                   pl.BlockSpec((1, tb, DT_W), lambda b, i: (b, i, 0))],
        out_shape=[jax.ShapeDtypeStruct((bsz, t, ncols), BF16),
                   jax.ShapeDtypeStruct((bsz, t, DT_W), F32)],
        scratch_shapes=[pltpu.VMEM((PROJ_SUB, D_MODEL), BF16),
                        pltpu.VMEM((conv_rows, PROJ_TN), F32)],
        compiler_params=pltpu.CompilerParams(
            dimension_semantics=("parallel", "parallel"),
            vmem_limit_bytes=VMEM_LIMIT),
        name="inproj",
    )(xs, shift, scale, w_all, cw_all, cb_all, w_dt2, dt_bias)


def _ssd_kernel(*refs, direction, mode, nsteps, nsub, nbatch):
    if mode == "full":
        (xbc_ref, dt_ref, alog_ref, h0_ref, hexp_ref, u_ref, lw_ref, lgb_ref, lam_ref, lh0_ref,
         dskip_ref, yin_ref, hin_ref, g_ref,
         y_ref, hout_ref, s_scr, e_scr, lc_scr, y_scr, cb_scr) = refs
    elif mode == "inter":
        (xbc_ref, dt_ref, alog_ref, h0_ref, hexp_ref, u_ref, lw_ref, lgb_ref, lam_ref, lh0_ref,
         y_ref, hout_ref, s_scr, e_scr, lc_scr) = refs
    else:
        xbc_ref, dt_ref, alog_ref, h0_ref, hexp_ref, sfin_ref, s_scr, e_scr = refs
    step = pl.program_id(1)
    L = SSD_CHUNK
    fwd = direction == "fwd"
    with_lru = mode != "state"
    groups = [slice(g * SSD_GROUP_W, (g + 1) * SSD_GROUP_W) for g in range(SSD_GROUPS)]
    bcols = [slice(SSD_D_INNER + g * SSD_STATE, SSD_D_INNER + (g + 1) * SSD_STATE) for g in range(SSD_GROUPS)]
    ccols = [slice(SSD_XB + g * SSD_STATE, SSD_XB + (g + 1) * SSD_STATE) for g in range(SSD_GROUPS)]

    @pl.when(step == 0)
    def _():
        s_scr[...] = h0_ref[...]
        if with_lru:
            for bb in range(nbatch):
                lc_scr[bb] = jnp.broadcast_to(lh0_ref[bb], (SUBLANES, LRU_WIDTH))

    if with_lru:
        ldir = 0 if fwd else 1
        chalf, oks = _lru_consts(lam_ref[ldir:ldir + 1, :], fwd)
        ldcols = slice(ldir * 2 * LRU_BLOCK_W, (ldir + 1) * 2 * LRU_BLOCK_W)

    def lru_tiles(bb, rows, ks):
        if not with_lru:
            return
        for k in ks:
            cols = slice(k * LRU_BLOCK_W, (k + 1) * LRU_BLOCK_W)
            hv, c = _lru_block(u_ref[bb, rows, cols], lw_ref[k, :, ldcols], lgb_ref[k:k + 1, ldcols],
                               chalf[:, cols], oks, lc_scr[bb, :, cols], fwd)
            lc_scr[bb, :, cols] = c
            if mode == "full":
                hv = (hv + hin_ref[bb, rows, cols].astype(F32)) * g_ref[bb, rows, cols].astype(F32)
            hout_ref[bb, rows, cols] = hv.astype(hout_ref.dtype)

    a_neg = -jnp.exp(alog_ref[...]) * LOG2E
    ri = lax.broadcasted_iota(jnp.int32, (L, L), 0)
    ci = lax.broadcasted_iota(jnp.int32, (L, L), 1)
    tril = jnp.where(ri >= ci, 1.0, 0.0).astype(F32)
    triu = jnp.where(ri <= ci, 1.0, 0.0).astype(F32)

    def expand(vals, col0):
        hi, lo = _split_bf16(vals[:, col0:col0 + SSD_HEADS])
        return jnp.dot(jnp.concatenate([hi, lo], axis=1), hexp_ref[...], preferred_element_type=F32)

    def chunk(i, bb):
        cidx = i if fwd else nsub - 1 - i
        rows = pl.ds(cidx * L, L)
        es = e_scr.at[bb, i]
        ss = s_scr.at[bb]
        if mode == "full":
            ys = y_scr.at[bb, i]
            cs = cb_scr.at[bb, i]
        dt = dt_ref[bb, rows, :]
        dta = dt * a_neg
        cum = jnp.dot(tril if fwd else triu, dta, precision=HIGHEST, preferred_element_type=F32)
        total = cum[L - 1:L, :] if fwd else cum[0:1, :]
        w_state = dt * jnp.exp2(total - cum)
        dec_tot = jnp.broadcast_to(jnp.exp2(total), (SUBLANES, DT_W))
        if mode == "state":
            stack = jnp.concatenate([w_state, dec_tot], axis=0)
        elif mode == "inter":
            stack = jnp.concatenate([w_state, dec_tot, jnp.exp2(cum)], axis=0)
        else:
            stack = jnp.concatenate([w_state, dec_tot, cum], axis=0)
        lru_tiles(bb, rows, (0, 1))
        yield
        es[...] = expand(stack, 0 if fwd else SSD_HEADS)
        r_dec = L
        r_ein = L + SUBLANES
        lru_tiles(bb, rows, (2, 3))
        yield

        if mode != "state":
            for g, gx in enumerate(groups):
                cg = xbc_ref[bb, rows, ccols[g]]
                e_in = es[r_ein:r_ein + L, gx]
                if mode == "full":
                    e_in = jnp.exp2(e_in)
                y_g = e_in * jnp.dot(cg, ss[:, gx].astype(BF16), preferred_element_type=F32)
                if mode == "inter":
                    y_ref[bb, rows, gx] = y_g.astype(y_ref.dtype)
                else:
                    xg = xbc_ref[bb, rows, gx].astype(F32)
                    ys[:, gx] = y_g + yin_ref[bb, rows, gx].astype(F32) + dskip_ref[:, gx] * xg

        lru_tiles(bb, rows, (4, 5))
        yield
        for g, gx in enumerate(groups):
            bg = xbc_ref[bb, rows, bcols[g]]
            xw = (xbc_ref[bb, rows, gx].astype(F32) * es[0:L, gx]).astype(BF16)
            upd = lax.dot_general(bg, xw, (((0,), (0,)), ((), ())), preferred_element_type=F32)
            ss[:, gx] = es[r_dec:r_dec + 1, gx] * ss[:, gx] + upd
        lru_tiles(bb, rows, (6, 7))
        yield

        if mode == "full":
            P = SSD_HEADDIM
            half = L // 2
            cum_b = jnp.dot(triu, dta, precision=HIGHEST, preferred_element_type=F32)
            cs[...] = expand(cum_b, SSD_HEADS)
            dta_t = dta.T
            ldt_t = jnp.log2(dt.T)
            row_f = ldt_t - jnp.dot(dta_t, triu, precision=HIGHEST, preferred_element_type=F32)
            row_b = ldt_t - jnp.dot(dta_t, tril, precision=HIGHEST, preferred_element_type=F32)
            ti = lax.broadcasted_iota(jnp.int32, (half, 2 * P), 0)
            tj = lax.broadcasted_iota(jnp.int32, (half, 2 * P), 1) % P
            lo_tri = ti >= tj
            up_tri = ti <= tj
            lane_a = lax.broadcasted_iota(jnp.int32, (half, 2 * P), 1) < P
            for g in range(SSD_GROUPS):
                bg = xbc_ref[bb, rows, bcols[g]]
                cg = xbc_ref[bb, rows, ccols[g]]
                b4 = jnp.concatenate([bg[:half, :], bg[:half, :], bg[half:, :], bg[half:, :]], axis=0)
                cb4 = lax.dot_general(cg, b4, (((1,), (1,)), ((), ())), preferred_element_type=F32)
                for pp in range(SSD_HPG // 2):
                    ha = g * SSD_HPG + 2 * pp
                    hb = SSD_HEADS + ha
                    px = slice(ha * P, (ha + 2) * P)
                    col_f = es[r_ein:r_ein + L, px]
                    col_b = cs[:, px]
                    xp = xbc_ref[bb, rows, px]
                    ms, rhss = [], []
                    for jh in range(2):
                        js = slice(jh * half, (jh + 1) * half)
                        rf = jnp.concatenate([row_f[ha:ha + 1, js], row_f[ha + 1:ha + 2, js]], axis=1)
                        rb = jnp.concatenate([row_b[hb:hb + 1, js], row_b[hb + 1:hb + 2, js]], axis=1)
                        dg = slice(jh * half, (jh + 1) * half)
                        m_diag = (jnp.where(lo_tri, jnp.exp2(col_f[dg, :] + rf), 0.0)
                                  + jnp.where(up_tri, jnp.exp2(col_b[dg, :] + rb), 0.0))
                        if jh == 0:
                            m = jnp.concatenate([m_diag, jnp.exp2(col_f[half:, :] + rf)], axis=0)
                        else:
                            m = jnp.concatenate([jnp.exp2(col_b[:half, :] + rb), m_diag], axis=0)
                        ms.append((m * cb4[:, jh * 2 * P:(jh + 1) * 2 * P]).astype(BF16))
                        xj = xp[js, :]
                        zero = jnp.zeros_like(xj)
                        rhss += [jnp.where(lane_a, xj, zero), jnp.where(lane_a, zero, xj)]
                    y2 = jnp.dot(jnp.concatenate(ms, axis=1), jnp.concatenate(rhss, axis=0),
                                 preferred_element_type=F32)
                    y_ref[bb, rows, px] = (ys[:, px] + y2).astype(y_ref.dtype)
                yield

    for i in range(nsub):
        for _ in zip(*[chunk(i, bb) for bb in range(nbatch)]):
            pass

    if mode == "state":
        @pl.when(step == nsteps - 1)
        def _():
            sfin_ref[...] = s_scr[...]


def _ssd_sweep(xbc, dt, alog_row, h0, hexp, *, direction, mode, xbc_width, lru=None, dskip=None, yin=None,
               hin=None):
    bsz, t, _ = xbc.shape
    tb = min(t, SSD_TB)
    nb = SSD_NB if bsz % SSD_NB == 0 else 1
    nsteps = t // tb
    hp = SSD_D_INNER
    w = LRU_WIDTH
    row = (lambda c: c) if direction == "fwd" else (lambda c: nsteps - 1 - c)
    cmap = lambda b, c: (b, row(c), 0)
    const = lambda shape: pl.BlockSpec(shape, lambda b, c: (0,) * len(shape))
    in_specs = [pl.BlockSpec((nb, tb, xbc_width), cmap),
                pl.BlockSpec((nb, tb, DT_W), cmap),
                const((1, DT_W)),
                pl.BlockSpec((nb, SSD_STATE, hp), lambda b, c: (b, 0, 0)),
                const((2 * SSD_HEADS, hp))]
    args = [xbc, dt, alog_row, h0, jnp.concatenate([hexp, hexp], axis=0)]
    stack_rows = (SSD_CHUNK + SUBLANES) if mode == "state" else (2 * SSD_CHUNK + SUBLANES)
    nsub = tb // SSD_CHUNK
    scratch = [pltpu.VMEM((nb, SSD_STATE, hp), F32), pltpu.VMEM((nb, nsub, stack_rows, hp), F32)]
    if mode != "state":
        proj, u_col, g_col, w_blk, gb_blk, lam, lh0 = lru
        in_specs += [pl.BlockSpec((nb, tb, w), lambda b, c: (b, row(c), u_col)),
                     const((LRU_BLOCKS, LRU_BLOCK_W, 4 * LRU_BLOCK_W)),
                     const((LRU_BLOCKS, 4 * LRU_BLOCK_W)),
                     const((2, w)),
                     pl.BlockSpec((nb, 1, w), lambda b, c: (b, 0, 0))]
        args += [proj, w_blk, gb_blk, lam, lh0]
        scratch.append(pltpu.VMEM((nb, SUBLANES, w), F32))
    if mode == "full":
        scratch += [pltpu.VMEM((nb, nsub, SSD_CHUNK, hp), F32), pltpu.VMEM((nb, nsub, SSD_CHUNK, hp), F32)]
        in_specs += [const((1, hp)), pl.BlockSpec((nb, tb, hp), cmap), pl.BlockSpec((nb, tb, w), cmap),
                     pl.BlockSpec((nb, tb, w), lambda b, c: (b, row(c), g_col))]
        args += [dskip, yin, hin, proj]
    if mode == "state":
        out_specs = pl.BlockSpec((nb, SSD_STATE, hp), lambda b, c: (b, 0, 0))
        out_shape = jax.ShapeDtypeStruct((bsz, SSD_STATE, hp), F32)
    else:
        out_specs = [pl.BlockSpec((nb, tb, hp), cmap), pl.BlockSpec((nb, tb, w), cmap)]
        out_shape = [jax.ShapeDtypeStruct((bsz, t, hp), BF16), jax.ShapeDtypeStruct((bsz, t, w), BF16)]
    kern = functools.partial(_ssd_kernel, direction=direction, mode=mode, nsteps=nsteps, nsub=nsub, nbatch=nb)
    return pl.pallas_call(
        kern,
        grid=(bsz // nb, nsteps),
        in_specs=in_specs,
        out_specs=out_specs,
        out_shape=out_shape,
        scratch_shapes=scratch,
        compiler_params=pltpu.CompilerParams(
            dimension_semantics=("parallel", "arbitrary"),
            vmem_limit_bytes=VMEM_LIMIT),
        name=f"ssd_{direction}_{mode}",
    )(*args)


def _lru_consts(lam_row, fwd):
    chalf = (-0.5 * LRU_C) * _softplus(-lam_row)
    sub_idx = lax.broadcasted_iota(jnp.int32, (LRU_VT, LRU_BLOCK_W), 0) % SUBLANES
    oks = {s: (sub_idx >= s) if fwd else (sub_idx < SUBLANES - s) for s in (1, 2, 4)}
    return chalf, oks


def _lru_block(ub, wk, gbk, chalf_k, oks, c, fwd):
    bw = LRU_BLOCK_W
    pre = jnp.dot(ub, wk, preferred_element_type=F32) + gbk
    nsl = LRU_TC // LRU_VT
    hs = [None] * nsl
    for si in (range(nsl) if fwd else range(nsl - 1, -1, -1)):
        rs = slice(si * LRU_VT, (si + 1) * LRU_VT)
        tr = jnp.tanh(pre[rs, :bw])
        ti = jnp.tanh(pre[rs, bw:])
        log_a = tr * chalf_k + chalf_k
        a = jnp.exp(log_a)
        th = jnp.tanh(log_a)
        b = jnp.sqrt(-0.5 * th / (1.0 - th)) * ((ti + 1.0) * ub[rs, :].astype(F32))
        for s in (1, 2, 4):
            ok = oks[s]
            shift = s if fwd else SUBLANES - s
            b = b + jnp.where(ok, a, 0.0) * _group_roll(b, shift)
            a = a * jnp.where(ok, _group_roll(a, shift), 1.0)
        ngrp = LRU_VT // SUBLANES
        hg_list = [None] * ngrp
        last = SUBLANES - 1 if fwd else 0
        for gi in (range(ngrp) if fwd else range(ngrp - 1, -1, -1)):
            sl = slice(gi * SUBLANES, (gi + 1) * SUBLANES)
            hg = b[sl, :] + a[sl, :] * c
            hg_list[gi] = hg
            c = jnp.broadcast_to(hg[last:last + 1, :], (SUBLANES, bw))
        hs[si] = jnp.concatenate(hg_list, axis=0)
    return jnp.concatenate(hs, axis=0), c


def _lru_state_kernel(u_ref, w_ref, gb_ref, lam_ref, sf_ref, sb_ref, *, t):
    bw = LRU_BLOCK_W
    nchunk = t // LRU_TC
    for d in range(2):
        fwd = d == 0
        chalf, oks = _lru_consts(lam_ref[d:d + 1, :], fwd)
        dcols = slice(d * 2 * bw, (d + 1) * 2 * bw)

        def chunk(i, carry, fwd=fwd, chalf=chalf, oks=oks, dcols=dcols):
            cidx = i if fwd else nchunk - 1 - i
            rows = pl.ds(pl.multiple_of(cidx * LRU_TC, LRU_TC), LRU_TC)
            out = []
            for k in range(LRU_BLOCKS):
                cols = slice(k * bw, (k + 1) * bw)
                _, c = _lru_block(u_ref[0, rows, cols], w_ref[k, :, dcols], gb_ref[k:k + 1, dcols],
                                  chalf[:, cols], oks, carry[k], fwd)
                out.append(c)
            return tuple(out)

        init = tuple(jnp.zeros((SUBLANES, bw), F32) for _ in range(LRU_BLOCKS))
        fin = lax.fori_loop(0, nchunk, chunk, init)
        for k in range(LRU_BLOCKS):
            (sf_ref if fwd else sb_ref)[0, :, k * bw:(k + 1) * bw] = fin[k][0:1, :]


def _lru_state(u_src, u_col, w_blk, gb_blk, lam):
    bsz, t, _ = u_src.shape
    w = LRU_WIDTH
    return pl.pallas_call(
        functools.partial(_lru_state_kernel, t=t),
        grid=(bsz,),
        in_specs=[pl.BlockSpec((1, t, w), lambda b: (b, 0, u_col)),
                  pl.BlockSpec((LRU_BLOCKS, LRU_BLOCK_W, 4 * LRU_BLOCK_W), lambda b: (0, 0, 0)),
                  pl.BlockSpec((LRU_BLOCKS, 4 * LRU_BLOCK_W), lambda b: (0, 0)),
                  pl.BlockSpec((2, w), lambda b: (0, 0))],
        out_specs=[pl.BlockSpec((1, 1, w), lambda b: (b, 0, 0)),
                   pl.BlockSpec((1, 1, w), lambda b: (b, 0, 0))],
        out_shape=[jax.ShapeDtypeStruct((bsz, 1, w), F32),
                   jax.ShapeDtypeStruct((bsz, 1, w), F32)],
        compiler_params=pltpu.CompilerParams(
            dimension_semantics=("parallel",),
            vmem_limit_bytes=VMEM_LIMIT),
        name="lru_state",
    )(u_src, w_blk, gb_blk, lam)


def _post_kernel(x_ref, y_ref, z_ref, gate_ref, ylru_ref, mod_ref, nw_ref, wbs_ref, wbl_ref, wout_ref,
                 ln1g_ref, ln1b_ref, w1_ref, b1_ref, w2_ref, b2_ref, ln2g_ref, ln2b_ref, o_ref):
    d = D_MODEL
    x = x_ref[0]
    u = y_ref[0].astype(F32) * z_ref[0].astype(F32)
    parts = []
    for g in range(SSD_GROUPS):
        ug = u[:, g * SSD_GROUP_W:(g + 1) * SSD_GROUP_W]
        ms = jnp.mean(ug * ug, axis=-1, keepdims=True)
        parts.append(ug * lax.rsqrt(ms + RMS_EPS))
    un = (jnp.concatenate(parts, axis=1) * nw_ref[...]).astype(BF16)
    br_ssd = jnp.dot(un, wbs_ref[...], preferred_element_type=F32)
    br_lru = jnp.dot(ylru_ref[0], wbl_ref[...], preferred_element_type=F32)
    gates = gate_ref[0].astype(F32)
    merged = (gates[:, :d] * br_ssd + gates[:, d:] * br_lru).astype(BF16)
    x_mix = jnp.dot(merged, wout_ref[...], preferred_element_type=F32)
    mod = mod_ref[0]
    gate1 = mod[:, 2 * d:3 * d]
    shift2 = mod[:, 3 * d:4 * d]
    scale2 = mod[:, 4 * d:5 * d]
    gate2 = mod[:, 5 * d:6 * d]
    x1 = _layer_norm(DEEPNORM_ALPHA * x + gate1 * x_mix) * ln1g_ref[...] + ln1b_ref[...]
    h2 = (_layer_norm(x1) * (1.0 + scale2) + shift2).astype(BF16)
    hid = jnp.dot(h2, w1_ref[...], preferred_element_type=F32) + b1_ref[...]
    hid = jnp.maximum(hid, 0.0)
    hid = (hid * hid).astype(BF16)
    mlp = jnp.dot(hid, w2_ref[...], preferred_element_type=F32) + b2_ref[...]
    o_ref[0] = _layer_norm(DEEPNORM_ALPHA * x1 + gate2 * mlp) * ln2g_ref[...] + ln2b_ref[...]


def _const_spec(shape):
    nd = len(shape)
    return pl.BlockSpec(shape, lambda b, i: (0,) * nd, pipeline_mode=pl.Buffered(1))


def _post(x, y_ssd, proj, y_lru, mod, norm_w, w_br_ssd, w_br_lru, w_out, ln1_g, ln1_b,
          w1, b1, w2, b2, ln2_g, ln2_b, *, tb, z_col, gate_col):
    bsz, t, d = x.shape
    di = SSD_D_INNER
    row = lambda v: v.reshape(1, -1)
    return pl.pallas_call(
        _post_kernel,
        grid=(bsz, t // tb),
        in_specs=[pl.BlockSpec((1, tb, d), lambda b, i: (b, i, 0)),
                  pl.BlockSpec((1, tb, di), lambda b, i: (b, i, 0)),
                  pl.BlockSpec((1, tb, di), lambda b, i: (b, i, z_col)),
                  pl.BlockSpec((1, tb, di), lambda b, i: (b, i, gate_col)),
                  pl.BlockSpec((1, tb, d), lambda b, i: (b, i, 0)),
                  pl.BlockSpec((1, 1, N_MOD * d), lambda b, i: (b, 0, 0)),
                  _const_spec((1, di)),
                  _const_spec((di, d)),
                  _const_spec((d, d)),
                  _const_spec((d, d)),
                  _const_spec((1, d)),
                  _const_spec((1, d)),
                  _const_spec((d, MLP_HIDDEN)),
                  _const_spec((1, MLP_HIDDEN)),
                  _const_spec((MLP_HIDDEN, d)),
                  _const_spec((1, d)),
                  _const_spec((1, d)),
                  _const_spec((1, d))],
        out_specs=pl.BlockSpec((1, tb, d), lambda b, i: (b, i, 0)),
        out_shape=jax.ShapeDtypeStruct((bsz, t, d), F32),
        compiler_params=pltpu.CompilerParams(
            dimension_semantics=("parallel", "parallel"),
            vmem_limit_bytes=VMEM_LIMIT),
        name="post",
    )(x, y_ssd, proj, proj, y_lru, mod, row(norm_w), w_br_ssd.astype(BF16), w_br_lru.astype(BF16),
      w_out.astype(BF16), row(ln1_g), row(ln1_b), w1.astype(BF16), row(b1), w2.astype(BF16), row(b2),
      row(ln2_g), row(ln2_b))


def _pad_rows(v, n):
    return jnp.pad(v, ((0, n - v.shape[0]), (0, 0)))


def kernel(x, c, ctx, c_ctx, w_mod, b_mod, w_in, b_gate, ssd_conv_w, ssd_conv_b, ssd_dt_bias, ssd_a_log,
           ssd_d, ssd_norm_w, lru_conv_w, lru_conv_b, lru_wa, lru_ba, lru_wi, lru_bi, lru_lambda, w_br_ssd,
           w_br_lru, w_out, ln1_g, ln1_b, w_mlp1, b_mlp1, w_mlp2, b_mlp2, ln2_g, ln2_b):
    bsz, t, d = x.shape
    tctx = ctx.shape[1]
    l = 0
    wi = w_in[l]

    nrow = -(-(bsz + 1) // SUBLANES) * SUBLANES
    cvecs = _pad_rows(jnp.concatenate([c, c_ctx[None, :]], axis=0), nrow)
    mods = _modulation(cvecs, w_mod[l], b_mod[l])
    mod_x = mods[:bsz].reshape(bsz, 1, N_MOD * d)
    shift_x, scale_x = mod_x[:, :, :d], mod_x[:, :, d:2 * d]
    shift_c = jnp.broadcast_to(mods[bsz:bsz + 1, :d].reshape(1, 1, d), (bsz, 1, d))
    scale_c = jnp.broadcast_to(mods[bsz:bsz + 1, d:2 * d].reshape(1, 1, d), (bsz, 1, d))

    zc = lambda n: jnp.zeros((CONV_K, n), F32)
    z1 = lambda n: jnp.zeros((n,), F32)
    w_x = jnp.concatenate([wi[:, :SSD_XB], wi[:, O_C:O_Z], wi[:, O_Z:O_LRU_GATE], wi[:, O_MERGE:],
                           wi[:, O_LRU:STATE_COLS], wi[:, O_LRU_GATE:O_MERGE]], axis=1).astype(BF16)
    cw_x = _pad_rows(jnp.concatenate([ssd_conv_w[l], zc(SSD_D_INNER), zc(2 * d), lru_conv_w[l], zc(LRU_WIDTH)],
                                     axis=1), SUBLANES)
    cb_x = jnp.concatenate([ssd_conv_b[l], z1(SSD_D_INNER), b_gate[l], lru_conv_b[l], z1(LRU_WIDTH)])[None, :]
    n_xbc, n_z, n_m, n_l = SSD_XBC // PROJ_TN, SSD_D_INNER // PROJ_TN, 2 * d // PROJ_TN, LRU_WIDTH // PROJ_TN
    segs_x, lo = [], 0
    for n, kind in ((n_xbc, "conv_silu"), (n_z, "silu"), (n_m, "sigmoid_bias"), (n_l, "conv"), (n_l, "gelu")):
        segs_x.append((lo, lo + n, kind))
        lo += n
    z_col = SSD_XBC // SSD_D_INNER
    gate_col = z_col + 1
    lru_col = (SSD_XBC + 2 * SSD_D_INNER) // LRU_WIDTH
    lrug_col = lru_col + 1

    w_c = jnp.concatenate([wi[:, :SSD_XB], wi[:, O_LRU:STATE_COLS]], axis=1).astype(BF16)
    cw_c = _pad_rows(jnp.concatenate([ssd_conv_w[l][:, :SSD_XB], lru_conv_w[l]], axis=1), SUBLANES)
    cb_c = jnp.concatenate([ssd_conv_b[l][:SSD_XB], lru_conv_b[l]])[None, :]
    segs_c = [(0, SSD_XB // PROJ_TN, "conv_silu"), (SSD_XB // PROJ_TN, SSD_XB // PROJ_TN + n_l, "conv")]
    lru_col_c = SSD_XB // LRU_WIDTH

    w_dt = jnp.pad(wi[:, O_DT:O_LRU], ((0, 0), (0, DT_W - SSD_DT)))
    w_dt_hi = w_dt.astype(BF16)
    w_dt_lo = (w_dt - w_dt_hi.astype(F32)).astype(BF16)
    w_dt2 = jnp.concatenate([w_dt_hi, w_dt_lo], axis=1)
    dt_bias = jnp.pad(ssd_dt_bias[l].reshape(1, SSD_DT), ((0, 0), (0, DT_W - SSD_DT)))
    alog_row = jnp.pad(ssd_a_log[l].reshape(1, SSD_DT), ((0, 0), (0, DT_W - SSD_DT)))
    dskip = jnp.repeat(ssd_d[l], SSD_HEADDIM)[None, :]
    hexp = jnp.repeat(jnp.eye(SSD_HEADS, dtype=BF16), SSD_HEADDIM, axis=1)

    w_blk = (0.5 * jnp.concatenate([lru_wa[l, 0], lru_wi[l, 0], lru_wa[l, 1], lru_wi[l, 1]], axis=-1)).astype(BF16)
    blk = lambda v: v.reshape(LRU_BLOCKS, LRU_BLOCK_W)
    gb_blk = 0.5 * jnp.concatenate([blk(lru_ba[l, 0]), blk(lru_bi[l, 0]), blk(lru_ba[l, 1]), blk(lru_bi[l, 1])],
                                   axis=-1)
    lam = lru_lambda[l]

    proj_c, dt_c = _inproj(ctx, shift_c, scale_c, w_c, cw_c, cb_c, w_dt2, dt_bias,
                           segs=segs_c, row_len=tctx, tb=tctx)
    zero_s = jnp.zeros((bsz, SSD_STATE, SSD_D_INNER), F32)
    s_f = _ssd_sweep(proj_c, dt_c, alog_row, zero_s, hexp, direction="fwd", mode="state", xbc_width=SSD_XB)
    s_b = _ssd_sweep(proj_c, dt_c, alog_row, zero_s, hexp, direction="bwd", mode="state", xbc_width=SSD_XB)
    l_f, l_b = _lru_state(proj_c, lru_col_c, w_blk, gb_blk, lam)

    proj_x, dt_x = _inproj(x, shift_x, scale_x, w_x, cw_x, cb_x, w_dt2, dt_bias,
                           segs=segs_x, row_len=GRID_W, tb=min(t, PROJ_TB))
    y_b, h_b = _ssd_sweep(proj_x, dt_x, alog_row, s_b, hexp, direction="bwd", mode="inter", xbc_width=SSD_XBC,
                          lru=(proj_x, lru_col, lrug_col, w_blk, gb_blk, lam, l_b))
    y_ssd, y_lru = _ssd_sweep(proj_x, dt_x, alog_row, s_f, hexp, direction="fwd", mode="full", xbc_width=SSD_XBC,
                              lru=(proj_x, lru_col, lrug_col, w_blk, gb_blk, lam, l_f),
                              dskip=dskip, yin=y_b, hin=h_b)
    return _post(x, y_ssd, proj_x, y_lru, mod_x, ssd_norm_w[l], w_br_ssd[l], w_br_lru[l], w_out[l],
                 ln1_g[l], ln1_b[l], w_mlp1[l], b_mlp1[l], w_mlp2[l], b_mlp2[l], ln2_g[l], ln2_b[l],
                 tb=min(t, POST_TB), z_col=z_col, gate_col=gate_col)
```

```python
import functools

import jax
import jax.numpy as jnp
from jax import lax
from jax.experimental import pallas as pl
from jax.experimental.pallas import tpu as pltpu

F32 = jnp.float32
BF16 = jnp.bfloat16
HIGHEST = lax.Precision.HIGHEST

D_MODEL = 1024
GRID_W = 64
SSD_D_INNER = 2 * D_MODEL
SSD_HEADDIM = 64
SSD_HEADS = SSD_D_INNER // SSD_HEADDIM
SSD_GROUPS = 8
SSD_HPG = SSD_HEADS // SSD_GROUPS
SSD_STATE = 128
SSD_CHUNK = 128
SSD_GROUP_W = SSD_HPG * SSD_HEADDIM
CONV_K = 4
LRU_WIDTH = D_MODEL
LRU_BLOCKS = 8
LRU_BLOCK_W = LRU_WIDTH // LRU_BLOCKS
LRU_C = 8.0
MLP_HIDDEN = 4 * D_MODEL
N_MOD = 6
DEPTH = 1
DEEPNORM_ALPHA = (2 * DEPTH) ** 0.25
LN_EPS = 1e-6
RMS_EPS = 1e-5
LOG2E = 1.4426950408889634
assert SSD_CHUNK // 2 == SSD_HEADDIM

SSD_BC_W = SSD_GROUPS * SSD_STATE
SSD_XB = SSD_D_INNER + SSD_BC_W
SSD_XBC = SSD_D_INNER + 2 * SSD_BC_W
SSD_DT = 2 * SSD_HEADS
O_DT = SSD_XB
O_LRU = O_DT + SSD_DT
STATE_COLS = O_LRU + LRU_WIDTH
O_C = STATE_COLS
O_Z = O_C + SSD_BC_W
O_LRU_GATE = O_Z + SSD_D_INNER
O_MERGE = O_LRU_GATE + LRU_WIDTH

LANES = 128
SUBLANES = 8
VMEM_LIMIT = 56 * 1024 * 1024

PROJ_TN = 1024
PROJ_SUB = 256
PROJ_TB = 512
CONV_GAP = SUBLANES
DT_W = LANES
SSD_TB = 256
SSD_NB = 2
LRU_TC = 128
LRU_VT = 32
POST_TB = 256


def _silu(v):
    return v / (1.0 + jnp.exp(-v))


def _sigmoid(v):
    return 1.0 / (1.0 + jnp.exp(-v))


def _softplus(v):
    return jnp.maximum(v, 0.0) + jnp.log1p(jnp.exp(-jnp.abs(v)))


def _layer_norm(v):
    mu = jnp.mean(v, axis=-1, keepdims=True)
    vc = v - mu
    var = jnp.mean(vc * vc, axis=-1, keepdims=True)
    return vc * lax.rsqrt(var + LN_EPS)


def _group_roll(v, shift):
    rows, width = v.shape
    v3 = v.reshape(rows // SUBLANES, SUBLANES, width)
    return pltpu.roll(v3, shift, 1).reshape(rows, width)


def _split3_bf16(v):
    p1 = v.astype(BF16)
    r1 = v - p1.astype(F32)
    p2 = r1.astype(BF16)
    p3 = (r1 - p2.astype(F32)).astype(BF16)
    return p1, p2, p3


def _tri_rows(tri, v):
    w = v.shape[1]
    out = jnp.dot(tri, jnp.concatenate(_split3_bf16(v), axis=1), preferred_element_type=F32)
    return out[:, :w] + out[:, w:2 * w] + out[:, 2 * w:]


def _tri_cols(v, tri):
    h = v.shape[0]
    out = jnp.dot(jnp.concatenate(_split3_bf16(v), axis=0), tri, preferred_element_type=F32)
    return out[:h, :] + out[h:2 * h, :] + out[2 * h:, :]


def _split_bf16(v):
    hi = v.astype(BF16)
    lo = (v - hi.astype(F32)).astype(BF16)
    return hi, lo


def _mod_kernel(c_ref, w_ref, b_ref, o_ref):
    s = _silu(c_ref[...])
    o_ref[...] = jnp.dot(s, w_ref[...], precision=HIGHEST, preferred_element_type=F32) + b_ref[...]


def _modulation(cvecs, w_mod, b_mod):
    rows, ncols = cvecs.shape[0], w_mod.shape[1]
    tn = 512
    return pl.pallas_call(
        _mod_kernel,
        grid=(ncols // tn,),
        in_specs=[pl.BlockSpec((rows, D_MODEL), lambda j: (0, 0)),
                  pl.BlockSpec((D_MODEL, tn), lambda j: (0, j)),
                  pl.BlockSpec((1, tn), lambda j: (0, j))],
        out_specs=pl.BlockSpec((rows, tn), lambda j: (0, j)),
        out_shape=jax.ShapeDtypeStruct((rows, ncols), F32),
        name="modulation",
    )(cvecs, w_mod, b_mod.reshape(1, ncols))


def _inproj_kernel(x_ref, shift_ref, scale_ref, w_ref, cw_ref, cb_ref, wdt_ref, dtb_ref,
                   p_ref, dt_ref, h_scr, c_scr, *, segs, row_len, tb):
    sub = PROJ_SUB
    npiece = sub // row_len
    pitch = row_len + CONV_GAP
    zgap = jnp.zeros((CONV_GAP, PROJ_TN), F32)
    for r in range(npiece + 1):
        c_scr[r * pitch:r * pitch + CONV_GAP, :] = zgap

    def sub_block(s, carry):
        rows = pl.ds(pl.multiple_of(s * sub, sub), sub)
        h = _layer_norm(x_ref[0, rows, :]) * (1.0 + scale_ref[0]) + shift_ref[0]
        hb = h.astype(BF16)
        h_scr[...] = hb
        dt2 = jnp.dot(hb, wdt_ref[...], preferred_element_type=F32)
        dt_ref[0, rows, :] = _softplus(dt2[:, :DT_W] + dt2[:, DT_W:] + dtb_ref[...])
        for lo, hi, kind in segs:
            for j in range(lo, hi):
                cols = slice(j * PROJ_TN, (j + 1) * PROJ_TN)
                acc = jnp.dot(h_scr[...], w_ref[:, cols], preferred_element_type=F32)
                cb = cb_ref[:, cols]
                if kind in ("conv_silu", "conv"):
                    cw = cw_ref[:, cols]
                    for r in range(npiece):
                        base = CONV_GAP + r * pitch
                        c_scr[base:base + row_len, :] = acc[r * row_len:(r + 1) * row_len, :]
                    pieces = []
                    for r in range(npiece):
                        base = CONV_GAP + r * pitch
                        o = cb + acc[r * row_len:(r + 1) * row_len, :] * cw[2:3]
                        o = o + c_scr[base - 2:base - 2 + row_len, :] * cw[0:1]
                        o = o + c_scr[base - 1:base - 1 + row_len, :] * cw[1:2]
                        o = o + c_scr[base + 1:base + 1 + row_len, :] * cw[3:4]
                        pieces.append(o)
                    out = pieces[0] if npiece == 1 else jnp.concatenate(pieces, axis=0)
                    if kind == "conv_silu":
                        out = _silu(out)
                elif kind == "silu":
                    out = _silu(acc)
                elif kind == "gelu":
                    out = 0.5 * acc * (1.0 + jnp.tanh(0.7978845608028654 * (acc + 0.044715 * acc * acc * acc)))
                elif kind == "sigmoid_bias":
                    out = _sigmoid(acc + cb)
                else:
                    raise ValueError(kind)
                p_ref[0, rows, cols] = out.astype(BF16)
        return carry

    lax.fori_loop(0, tb // sub, sub_block, 0)


def _inproj(xs, shift, scale, w_all, cw_all, cb_all, w_dt2, dt_bias, *, segs, row_len, tb):
    bsz, t, _ = xs.shape
    ncols = w_all.shape[1]
    kern = functools.partial(_inproj_kernel, segs=segs, row_len=row_len, tb=tb)
    const = lambda shape: pl.BlockSpec(shape, lambda b, i: (0, 0), pipeline_mode=pl.Buffered(1))
    conv_rows = CONV_GAP + (PROJ_SUB // row_len) * (row_len + CONV_GAP)
    return pl.pallas_call(
        kern,
        grid=(bsz, t // tb),
        in_specs=[pl.BlockSpec((1, tb, D_MODEL), lambda b, i: (b, i, 0)),
                  pl.BlockSpec((1, 1, D_MODEL), lambda b, i: (b, 0, 0)),
                  pl.BlockSpec((1, 1, D_MODEL), lambda b, i: (b, 0, 0)),
                  const((D_MODEL, ncols)),
                  const((SUBLANES, ncols)),
                  const((1, ncols)),
                  const((D_MODEL, 2 * DT_W)),
                  const((1, DT_W))],
        out_specs=[pl.BlockSpec((1, tb, ncols), lambda b, i: (b, i, 0)),
                   pl.BlockSpec((1, tb, DT_W), lambda b, i: (b, i, 0))],
        out_shape=[jax.ShapeDtypeStruct((bsz, t, ncols), BF16),
                   jax.ShapeDtypeStruct((bsz, t, DT_W), F32)],
        scratch_shapes=[pltpu.VMEM((PROJ_SUB, D_MODEL), BF16),
                        pltpu.VMEM((conv_rows, PROJ_TN), F32)],
        compiler_params=pltpu.CompilerParams(
            dimension_semantics=("parallel", "parallel"),
            vmem_limit_bytes=VMEM_LIMIT),
        name="inproj",
    )(xs, shift, scale, w_all, cw_all, cb_all, w_dt2, dt_bias)


def _ssd_kernel(*refs, direction, mode, nsteps, nsub, nbatch):
    if mode == "full":
        (xbc_ref, dt_ref, alog_ref, h0_ref, hexp_ref, u_ref, lw_ref, lgb_ref, lam_ref, lh0_ref,
         dskip_ref, yin_ref, hin_ref, g_ref,
         y_ref, hout_ref, s_scr, e_scr, lc_scr, y_scr, cb_scr) = refs
    elif mode == "inter":
        (xbc_ref, dt_ref, alog_ref, h0_ref, hexp_ref, u_ref, lw_ref, lgb_ref, lam_ref, lh0_ref,
         y_ref, hout_ref, s_scr, e_scr, lc_scr) = refs
    else:
        xbc_ref, dt_ref, alog_ref, h0_ref, hexp_ref, sfin_ref, s_scr, e_scr = refs
    step = pl.program_id(1)
    L = SSD_CHUNK
    fwd = direction == "fwd"
    with_lru = mode != "state"
    groups = [slice(g * SSD_GROUP_W, (g + 1) * SSD_GROUP_W) for g in range(SSD_GROUPS)]
    bcols = [slice(SSD_D_INNER + g * SSD_STATE, SSD_D_INNER + (g + 1) * SSD_STATE) for g in range(SSD_GROUPS)]
    ccols = [slice(SSD_XB + g * SSD_STATE, SSD_XB + (g + 1) * SSD_STATE) for g in range(SSD_GROUPS)]

    @pl.when(step == 0)
    def _():
        s_scr[...] = h0_ref[...]
        if with_lru:
            for bb in range(nbatch):
                lc_scr[bb] = jnp.broadcast_to(lh0_ref[bb], (SUBLANES, LRU_WIDTH))

    if with_lru:
        ldir = 0 if fwd else 1
        chalf, oks = _lru_consts(lam_ref[ldir:ldir + 1, :], fwd)
        ldcols = slice(ldir * 2 * LRU_BLOCK_W, (ldir + 1) * 2 * LRU_BLOCK_W)

    def lru_tiles(bb, rows, ks):
        if not with_lru:
            return
        for k in ks:
            cols = slice(k * LRU_BLOCK_W, (k + 1) * LRU_BLOCK_W)
            hv, c = _lru_block(u_ref[bb, rows, cols], lw_ref[k, :, ldcols], lgb_ref[k:k + 1, ldcols],
                               chalf[:, cols], oks, lc_scr[bb, :, cols], fwd)
            lc_scr[bb, :, cols] = c
            if mode == "full":
                hv = (hv + hin_ref[bb, rows, cols].astype(F32)) * g_ref[bb, rows, cols].astype(F32)
            hout_ref[bb, rows, cols] = hv.astype(hout_ref.dtype)

    a_neg = -jnp.exp(alog_ref[...]) * LOG2E
    ri = lax.broadcasted_iota(jnp.int32, (L, L), 0)
    ci = lax.broadcasted_iota(jnp.int32, (L, L), 1)
    tril = jnp.where(ri >= ci, 1.0, 0.0).astype(BF16)
    triu = jnp.where(ri <= ci, 1.0, 0.0).astype(BF16)

    def expand(vals, col0):
        hi, lo = _split_bf16(vals[:, col0:col0 + SSD_HEADS])
        return jnp.dot(jnp.concatenate([hi, lo], axis=1), hexp_ref[...], preferred_element_type=F32)

    def chunk(i, bb):
        cidx = i if fwd else nsub - 1 - i
        rows = pl.ds(cidx * L, L)
        es = e_scr.at[bb, i]
        ss = s_scr.at[bb]
        if mode == "full":
            ys = y_scr.at[bb, i]
            cs = cb_scr.at[bb, i]
        dt = dt_ref[bb, rows, :]
        dta = dt * a_neg
        cum = _tri_rows(tril if fwd else triu, dta)
        total = cum[L - 1:L, :] if fwd else cum[0:1, :]
        w_state = dt * jnp.exp2(total - cum)
        dec_tot = jnp.broadcast_to(jnp.exp2(total), (SUBLANES, DT_W))
        if mode == "state":
            stack = jnp.concatenate([w_state, dec_tot], axis=0)
        elif mode == "inter":
            stack = jnp.concatenate([w_state, dec_tot, jnp.exp2(cum)], axis=0)
        else:
            stack = jnp.concatenate([w_state, dec_tot, cum], axis=0)
        lru_tiles(bb, rows, (0, 1))
        yield
        es[...] = expand(stack, 0 if fwd else SSD_HEADS)
        r_dec = L
        r_ein = L + SUBLANES
        lru_tiles(bb, rows, (2, 3))
        if mode == "full":
            P = SSD_HEADDIM
            half = L // 2
            cum_b = _tri_rows(triu, dta)
            cs[...] = expand(cum_b, SSD_HEADS)
            dta_t = dta.T
            ldt_t = jnp.log2(dt.T)
            row_f = ldt_t - _tri_cols(dta_t, triu)
            row_b = ldt_t - _tri_cols(dta_t, tril)
            ti = lax.broadcasted_iota(jnp.int32, (half, 2 * P), 0)
            tj = lax.broadcasted_iota(jnp.int32, (half, 2 * P), 1) % P
            lo_tri = ti >= tj
            up_tri = ti <= tj
            lane_a = lax.broadcasted_iota(jnp.int32, (half, 2 * P), 1) < P

            def intra_group(g):
                bg = xbc_ref[bb, rows, bcols[g]]
                cg = xbc_ref[bb, rows, ccols[g]]
                b4 = jnp.concatenate([bg[:half, :], bg[:half, :], bg[half:, :], bg[half:, :]], axis=0)
                cb4 = lax.dot_general(cg, b4, (((1,), (1,)), ((), ())), preferred_element_type=F32)
                for pp in range(SSD_HPG // 2):
                    ha = g * SSD_HPG + 2 * pp
                    hb = SSD_HEADS + ha
                    px = slice(ha * P, (ha + 2) * P)
                    col_f = es[r_ein:r_ein + L, px]
                    col_b = cs[:, px]
                    xp = xbc_ref[bb, rows, px]
                    ms, rhss = [], []
                    for jh in range(2):
                        js = slice(jh * half, (jh + 1) * half)
                        rf = jnp.concatenate([row_f[ha:ha + 1, js], row_f[ha + 1:ha + 2, js]], axis=1)
                        rb = jnp.concatenate([row_b[hb:hb + 1, js], row_b[hb + 1:hb + 2, js]], axis=1)
                        dg = slice(jh * half, (jh + 1) * half)
                        m_diag = (jnp.where(lo_tri, jnp.exp2(col_f[dg, :] + rf), 0.0)
                                  + jnp.where(up_tri, jnp.exp2(col_b[dg, :] + rb), 0.0))
                        if jh == 0:
                            m = jnp.concatenate([m_diag, jnp.exp2(col_f[half:, :] + rf)], axis=0)
                        else:
                            m = jnp.concatenate([jnp.exp2(col_b[:half, :] + rb), m_diag], axis=0)
                        ms.append((m * cb4[:, jh * 2 * P:(jh + 1) * 2 * P]).astype(BF16))
                        xj = xp[js, :]
                        zero = jnp.zeros_like(xj)
                        rhss += [jnp.where(lane_a, xj, zero), jnp.where(lane_a, zero, xj)]
                    y2 = jnp.dot(jnp.concatenate(ms, axis=1), jnp.concatenate(rhss, axis=0),
                                 preferred_element_type=F32)
                    y_ref[bb, rows, px] = (ys[:, px] + y2).astype(y_ref.dtype)

        yield

        if mode != "state":
            for g, gx in enumerate(groups):
                cg = xbc_ref[bb, rows, ccols[g]]
                e_in = es[r_ein:r_ein + L, gx]
                if mode == "full":
                    e_in = jnp.exp2(e_in)
                y_g = e_in * jnp.dot(cg, ss[:, gx].astype(BF16), preferred_element_type=F32)
                if mode == "inter":
                    y_ref[bb, rows, gx] = y_g.astype(y_ref.dtype)
                else:
                    xg = xbc_ref[bb, rows, gx].astype(F32)
                    ys[:, gx] = y_g + yin_ref[bb, rows, gx].astype(F32) + dskip_ref[:, gx] * xg

        lru_tiles(bb, rows, (4, 5))
        yield
        if mode == "full":
            for g in range(SSD_GROUPS // 2):
                intra_group(g)
                yield
        for g, gx in enumerate(groups):
            bg = xbc_ref[bb, rows, bcols[g]]
            xw = (xbc_ref[bb, rows, gx].astype(F32) * es[0:L, gx]).astype(BF16)
            upd = lax.dot_general(bg, xw, (((0,), (0,)), ((), ())), preferred_element_type=F32)
            ss[:, gx] = es[r_dec:r_dec + 1, gx] * ss[:, gx] + upd
        lru_tiles(bb, rows, (6, 7))
        yield

        if mode == "full":
            for g in range(SSD_GROUPS // 2, SSD_GROUPS):
                intra_group(g)
                yield

    gens = [[chunk(i, bb) for bb in range(nbatch)] for i in range(nsub)]
    for _ in range(2):
        for i in range(nsub):
            for gen in gens[i]:
                next(gen)
    for i in range(nsub):
        for _ in zip(*gens[i]):
            pass

    if mode == "state":
        @pl.when(step == nsteps - 1)
        def _():
            sfin_ref[...] = s_scr[...]


def _ssd_sweep(xbc, dt, alog_row, h0, hexp, *, direction, mode, xbc_width, lru=None, dskip=None, yin=None,
               hin=None):
    bsz, t, _ = xbc.shape
    tb = min(t, SSD_TB)
    nb = SSD_NB if bsz % SSD_NB == 0 else 1
    nsteps = t // tb
    hp = SSD_D_INNER
    w = LRU_WIDTH
    row = (lambda c: c) if direction == "fwd" else (lambda c: nsteps - 1 - c)
    cmap = lambda b, c: (b, row(c), 0)
    const = lambda shape: pl.BlockSpec(shape, lambda b, c: (0,) * len(shape))
    in_specs = [pl.BlockSpec((nb, tb, xbc_width), cmap),
                pl.BlockSpec((nb, tb, DT_W), cmap),
                const((1, DT_W)),
                pl.BlockSpec((nb, SSD_STATE, hp), lambda b, c: (b, 0, 0)),
                const((2 * SSD_HEADS, hp))]
    args = [xbc, dt, alog_row, h0, jnp.concatenate([hexp, hexp], axis=0)]
    stack_rows = (SSD_CHUNK + SUBLANES) if mode == "state" else (2 * SSD_CHUNK + SUBLANES)
    nsub = tb // SSD_CHUNK
    scratch = [pltpu.VMEM((nb, SSD_STATE, hp), F32), pltpu.VMEM((nb, nsub, stack_rows, hp), F32)]
    if mode != "state":
        proj, u_col, g_col, w_blk, gb_blk, lam, lh0 = lru
        in_specs += [pl.BlockSpec((nb, tb, w), lambda b, c: (b, row(c), u_col)),
                     const((LRU_BLOCKS, LRU_BLOCK_W, 4 * LRU_BLOCK_W)),
                     const((LRU_BLOCKS, 4 * LRU_BLOCK_W)),
                     const((2, w)),
                     pl.BlockSpec((nb, 1, w), lambda b, c: (b, 0, 0))]
        args += [proj, w_blk, gb_blk, lam, lh0]
        scratch.append(pltpu.VMEM((nb, SUBLANES, w), F32))
    if mode == "full":
        scratch += [pltpu.VMEM((nb, nsub, SSD_CHUNK, hp), F32), pltpu.VMEM((nb, nsub, SSD_CHUNK, hp), F32)]
        in_specs += [const((1, hp)), pl.BlockSpec((nb, tb, hp), cmap), pl.BlockSpec((nb, tb, w), cmap),
                     pl.BlockSpec((nb, tb, w), lambda b, c: (b, row(c), g_col))]
        args += [dskip, yin, hin, proj]
    if mode == "state":
        out_specs = pl.BlockSpec((nb, SSD_STATE, hp), lambda b, c: (b, 0, 0))
        out_shape = jax.ShapeDtypeStruct((bsz, SSD_STATE, hp), F32)
    else:
        out_specs = [pl.BlockSpec((nb, tb, hp), cmap), pl.BlockSpec((nb, tb, w), cmap)]
        out_shape = [jax.ShapeDtypeStruct((bsz, t, hp), BF16), jax.ShapeDtypeStruct((bsz, t, w), BF16)]
    kern = functools.partial(_ssd_kernel, direction=direction, mode=mode, nsteps=nsteps, nsub=nsub, nbatch=nb)
    return pl.pallas_call(
        kern,
        grid=(bsz // nb, nsteps),
        in_specs=in_specs,
        out_specs=out_specs,
        out_shape=out_shape,
        scratch_shapes=scratch,
        compiler_params=pltpu.CompilerParams(
            dimension_semantics=("parallel", "arbitrary"),
            vmem_limit_bytes=VMEM_LIMIT),
        name=f"ssd_{direction}_{mode}",
    )(*args)


def _lru_consts(lam_row, fwd):
    chalf = (-0.5 * LRU_C) * _softplus(-lam_row)
    sub_idx = lax.broadcasted_iota(jnp.int32, (LRU_VT, LRU_BLOCK_W), 0) % SUBLANES
    oks = {s: (sub_idx >= s) if fwd else (sub_idx < SUBLANES - s) for s in (1, 2, 4)}
    return chalf, oks


def _lru_block(ub, wk, gbk, chalf_k, oks, c, fwd):
    bw = LRU_BLOCK_W
    pre = jnp.dot(ub, wk, preferred_element_type=F32) + gbk
    nsl = LRU_TC // LRU_VT
    hs = [None] * nsl
    for si in (range(nsl) if fwd else range(nsl - 1, -1, -1)):
        rs = slice(si * LRU_VT, (si + 1) * LRU_VT)
        tr = jnp.tanh(pre[rs, :bw])
        ti = jnp.tanh(pre[rs, bw:])
        log_a = tr * chalf_k + chalf_k
        a = jnp.exp(log_a)
        th = jnp.tanh(log_a)
        b = jnp.sqrt(-0.5 * th / (1.0 - th)) * ((ti + 1.0) * ub[rs, :].astype(F32))
        for s in (1, 2, 4):
            ok = oks[s]
            shift = s if fwd else SUBLANES - s
            b = b + jnp.where(ok, a, 0.0) * _group_roll(b, shift)
            a = a * jnp.where(ok, _group_roll(a, shift), 1.0)
        ngrp = LRU_VT // SUBLANES
        hg_list = [None] * ngrp
        last = SUBLANES - 1 if fwd else 0
        for gi in (range(ngrp) if fwd else range(ngrp - 1, -1, -1)):
            sl = slice(gi * SUBLANES, (gi + 1) * SUBLANES)
            hg = b[sl, :] + a[sl, :] * c
            hg_list[gi] = hg
            c = jnp.broadcast_to(hg[last:last + 1, :], (SUBLANES, bw))
        hs[si] = jnp.concatenate(hg_list, axis=0)
    return jnp.concatenate(hs, axis=0), c


def _lru_state_kernel(u_ref, w_ref, gb_ref, lam_ref, sf_ref, sb_ref, *, t):
    bw = LRU_BLOCK_W
    nchunk = t // LRU_TC
    for d in range(2):
        fwd = d == 0
        chalf, oks = _lru_consts(lam_ref[d:d + 1, :], fwd)
        dcols = slice(d * 2 * bw, (d + 1) * 2 * bw)

        def chunk(i, carry, fwd=fwd, chalf=chalf, oks=oks, dcols=dcols):
            cidx = i if fwd else nchunk - 1 - i
            rows = pl.ds(pl.multiple_of(cidx * LRU_TC, LRU_TC), LRU_TC)
            out = []
            for k in range(LRU_BLOCKS):
                cols = slice(k * bw, (k + 1) * bw)
                _, c = _lru_block(u_ref[0, rows, cols], w_ref[k, :, dcols], gb_ref[k:k + 1, dcols],
                                  chalf[:, cols], oks, carry[k], fwd)
                out.append(c)
            return tuple(out)

        init = tuple(jnp.zeros((SUBLANES, bw), F32) for _ in range(LRU_BLOCKS))
        fin = lax.fori_loop(0, nchunk, chunk, init)
        for k in range(LRU_BLOCKS):
            (sf_ref if fwd else sb_ref)[0, :, k * bw:(k + 1) * bw] = fin[k][0:1, :]


def _lru_state(u_src, u_col, w_blk, gb_blk, lam):
    bsz, t, _ = u_src.shape
    w = LRU_WIDTH
    return pl.pallas_call(
        functools.partial(_lru_state_kernel, t=t),
        grid=(bsz,),
        in_specs=[pl.BlockSpec((1, t, w), lambda b: (b, 0, u_col)),
                  pl.BlockSpec((LRU_BLOCKS, LRU_BLOCK_W, 4 * LRU_BLOCK_W), lambda b: (0, 0, 0)),
                  pl.BlockSpec((LRU_BLOCKS, 4 * LRU_BLOCK_W), lambda b: (0, 0)),
                  pl.BlockSpec((2, w), lambda b: (0, 0))],
        out_specs=[pl.BlockSpec((1, 1, w), lambda b: (b, 0, 0)),
                   pl.BlockSpec((1, 1, w), lambda b: (b, 0, 0))],
        out_shape=[jax.ShapeDtypeStruct((bsz, 1, w), F32),
                   jax.ShapeDtypeStruct((bsz, 1, w), F32)],
        compiler_params=pltpu.CompilerParams(
            dimension_semantics=("parallel",),
            vmem_limit_bytes=VMEM_LIMIT),
        name="lru_state",
    )(u_src, w_blk, gb_blk, lam)


def _post_kernel(x_ref, y_ref, z_ref, gate_ref, ylru_ref, mod_ref, nw_ref, wbs_ref, wbl_ref, wout_ref,
                 ln1g_ref, ln1b_ref, w1_ref, b1_ref, w2_ref, b2_ref, ln2g_ref, ln2b_ref, o_ref):
    d = D_MODEL
    x = x_ref[0]
    u = y_ref[0].astype(F32) * z_ref[0].astype(F32)
    parts = []
    for g in range(SSD_GROUPS):
        ug = u[:, g * SSD_GROUP_W:(g + 1) * SSD_GROUP_W]
        ms = jnp.mean(ug * ug, axis=-1, keepdims=True)
        parts.append(ug * lax.rsqrt(ms + RMS_EPS))
    un = (jnp.concatenate(parts, axis=1) * nw_ref[...]).astype(BF16)
    br_ssd = jnp.dot(un, wbs_ref[...], preferred_element_type=F32)
    br_lru = jnp.dot(ylru_ref[0], wbl_ref[...], preferred_element_type=F32)
    gates = gate_ref[0].astype(F32)
    merged = (gates[:, :d] * br_ssd + gates[:, d:] * br_lru).astype(BF16)
    x_mix = jnp.dot(merged, wout_ref[...], preferred_element_type=F32)
    mod = mod_ref[0]
    gate1 = mod[:, 2 * d:3 * d]
    shift2 = mod[:, 3 * d:4 * d]
    scale2 = mod[:, 4 * d:5 * d]
    gate2 = mod[:, 5 * d:6 * d]
    x1 = _layer_norm(DEEPNORM_ALPHA * x + gate1 * x_mix) * ln1g_ref[...] + ln1b_ref[...]
    h2 = (_layer_norm(x1) * (1.0 + scale2) + shift2).astype(BF16)
    hid = jnp.dot(h2, w1_ref[...], preferred_element_type=F32) + b1_ref[...]
    hid = jnp.maximum(hid, 0.0)
    hid = (hid * hid).astype(BF16)
    mlp = jnp.dot(hid, w2_ref[...], preferred_element_type=F32) + b2_ref[...]
    o_ref[0] = _layer_norm(DEEPNORM_ALPHA * x1 + gate2 * mlp) * ln2g_ref[...] + ln2b_ref[...]


def _const_spec(shape):
    nd = len(shape)
    return pl.BlockSpec(shape, lambda b, i: (0,) * nd, pipeline_mode=pl.Buffered(1))


def _post(x, y_ssd, proj, y_lru, mod, norm_w, w_br_ssd, w_br_lru, w_out, ln1_g, ln1_b,
          w1, b1, w2, b2, ln2_g, ln2_b, *, tb, z_col, gate_col):
    bsz, t, d = x.shape
    di = SSD_D_INNER
    row = lambda v: v.reshape(1, -1)
    return pl.pallas_call(
        _post_kernel,
        grid=(bsz, t // tb),
        in_specs=[pl.BlockSpec((1, tb, d), lambda b, i: (b, i, 0)),
                  pl.BlockSpec((1, tb, di), lambda b, i: (b, i, 0)),
                  pl.BlockSpec((1, tb, di), lambda b, i: (b, i, z_col)),
                  pl.BlockSpec((1, tb, di), lambda b, i: (b, i, gate_col)),
                  pl.BlockSpec((1, tb, d), lambda b, i: (b, i, 0)),
                  pl.BlockSpec((1, 1, N_MOD * d), lambda b, i: (b, 0, 0)),
                  _const_spec((1, di)),
                  _const_spec((di, d)),
                  _const_spec((d, d)),
                  _const_spec((d, d)),
                  _const_spec((1, d)),
                  _const_spec((1, d)),
                  _const_spec((d, MLP_HIDDEN)),
                  _const_spec((1, MLP_HIDDEN)),
                  _const_spec((MLP_HIDDEN, d)),
                  _const_spec((1, d)),
                  _const_spec((1, d)),
                  _const_spec((1, d))],
        out_specs=pl.BlockSpec((1, tb, d), lambda b, i: (b, i, 0)),
        out_shape=jax.ShapeDtypeStruct((bsz, t, d), F32),
        compiler_params=pltpu.CompilerParams(
            dimension_semantics=("parallel", "parallel"),
            vmem_limit_bytes=VMEM_LIMIT),
        name="post",
    )(x, y_ssd, proj, proj, y_lru, mod, row(norm_w), w_br_ssd.astype(BF16), w_br_lru.astype(BF16),
      w_out.astype(BF16), row(ln1_g), row(ln1_b), w1.astype(BF16), row(b1), w2.astype(BF16), row(b2),
      row(ln2_g), row(ln2_b))


def _pad_rows(v, n):
    return jnp.pad(v, ((0, n - v.shape[0]), (0, 0)))


def kernel(x, c, ctx, c_ctx, w_mod, b_mod, w_in, b_gate, ssd_conv_w, ssd_conv_b, ssd_dt_bias, ssd_a_log,
           ssd_d, ssd_norm_w, lru_conv_w, lru_conv_b, lru_wa, lru_ba, lru_wi, lru_bi, lru_lambda, w_br_ssd,
           w_br_lru, w_out, ln1_g, ln1_b, w_mlp1, b_mlp1, w_mlp2, b_mlp2, ln2_g, ln2_b):
    bsz, t, d = x.shape
    tctx = ctx.shape[1]
    l = 0
    wi = w_in[l]

    nrow = -(-(bsz + 1) // SUBLANES) * SUBLANES
    cvecs = _pad_rows(jnp.concatenate([c, c_ctx[None, :]], axis=0), nrow)
    mods = _modulation(cvecs, w_mod[l], b_mod[l])
    mod_x = mods[:bsz].reshape(bsz, 1, N_MOD * d)
    shift_x, scale_x = mod_x[:, :, :d], mod_x[:, :, d:2 * d]
    shift_c = jnp.broadcast_to(mods[bsz:bsz + 1, :d].reshape(1, 1, d), (bsz, 1, d))
    scale_c = jnp.broadcast_to(mods[bsz:bsz + 1, d:2 * d].reshape(1, 1, d), (bsz, 1, d))

    zc = lambda n: jnp.zeros((CONV_K, n), F32)
    z1 = lambda n: jnp.zeros((n,), F32)
    w_x = jnp.concatenate([wi[:, :SSD_XB], wi[:, O_C:O_Z], wi[:, O_Z:O_LRU_GATE], wi[:, O_MERGE:],
                           wi[:, O_LRU:STATE_COLS], wi[:, O_LRU_GATE:O_MERGE]], axis=1).astype(BF16)
    cw_x = _pad_rows(jnp.concatenate([ssd_conv_w[l], zc(SSD_D_INNER), zc(2 * d), lru_conv_w[l], zc(LRU_WIDTH)],
                                     axis=1), SUBLANES)
    cb_x = jnp.concatenate([ssd_conv_b[l], z1(SSD_D_INNER), b_gate[l], lru_conv_b[l], z1(LRU_WIDTH)])[None, :]
    n_xbc, n_z, n_m, n_l = SSD_XBC // PROJ_TN, SSD_D_INNER // PROJ_TN, 2 * d // PROJ_TN, LRU_WIDTH // PROJ_TN
    segs_x, lo = [], 0
    for n, kind in ((n_xbc, "conv_silu"), (n_z, "silu"), (n_m, "sigmoid_bias"), (n_l, "conv"), (n_l, "gelu")):
        segs_x.append((lo, lo + n, kind))
        lo += n
    z_col = SSD_XBC // SSD_D_INNER
    gate_col = z_col + 1
    lru_col = (SSD_XBC + 2 * SSD_D_INNER) // LRU_WIDTH
    lrug_col = lru_col + 1

    w_c = jnp.concatenate([wi[:, :SSD_XB], wi[:, O_LRU:STATE_COLS]], axis=1).astype(BF16)
    cw_c = _pad_rows(jnp.concatenate([ssd_conv_w[l][:, :SSD_XB], lru_conv_w[l]], axis=1), SUBLANES)
    cb_c = jnp.concatenate([ssd_conv_b[l][:SSD_XB], lru_conv_b[l]])[None, :]
    segs_c = [(0, SSD_XB // PROJ_TN, "conv_silu"), (SSD_XB // PROJ_TN, SSD_XB // PROJ_TN + n_l, "conv")]
    lru_col_c = SSD_XB // LRU_WIDTH

    w_dt = jnp.pad(wi[:, O_DT:O_LRU], ((0, 0), (0, DT_W - SSD_DT)))
    w_dt_hi = w_dt.astype(BF16)
    w_dt_lo = (w_dt - w_dt_hi.astype(F32)).astype(BF16)
    w_dt2 = jnp.concatenate([w_dt_hi, w_dt_lo], axis=1)
    dt_bias = jnp.pad(ssd_dt_bias[l].reshape(1, SSD_DT), ((0, 0), (0, DT_W - SSD_DT)))
    alog_row = jnp.pad(ssd_a_log[l].reshape(1, SSD_DT), ((0, 0), (0, DT_W - SSD_DT)))
    dskip = jnp.repeat(ssd_d[l], SSD_HEADDIM)[None, :]
    hexp = jnp.repeat(jnp.eye(SSD_HEADS, dtype=BF16), SSD_HEADDIM, axis=1)

    w_blk = (0.5 * jnp.concatenate([lru_wa[l, 0], lru_wi[l, 0], lru_wa[l, 1], lru_wi[l, 1]], axis=-1)).astype(BF16)
    blk = lambda v: v.reshape(LRU_BLOCKS, LRU_BLOCK_W)
    gb_blk = 0.5 * jnp.concatenate([blk(lru_ba[l, 0]), blk(lru_bi[l, 0]), blk(lru_ba[l, 1]), blk(lru_bi[l, 1])],
                                   axis=-1)
    lam = lru_lambda[l]

    proj_c, dt_c = _inproj(ctx, shift_c, scale_c, w_c, cw_c, cb_c, w_dt2, dt_bias,
                           segs=segs_c, row_len=tctx, tb=tctx)
    zero_s = jnp.zeros((bsz, SSD_STATE, SSD_D_INNER), F32)
    s_f = _ssd_sweep(proj_c, dt_c, alog_row, zero_s, hexp, direction="fwd", mode="state", xbc_width=SSD_XB)
    s_b = _ssd_sweep(proj_c, dt_c, alog_row, zero_s, hexp, direction="bwd", mode="state", xbc_width=SSD_XB)
    l_f, l_b = _lru_state(proj_c, lru_col_c, w_blk, gb_blk, lam)

    proj_x, dt_x = _inproj(x, shift_x, scale_x, w_x, cw_x, cb_x, w_dt2, dt_bias,
                           segs=segs_x, row_len=GRID_W, tb=min(t, PROJ_TB))
    y_b, h_b = _ssd_sweep(proj_x, dt_x, alog_row, s_b, hexp, direction="bwd", mode="inter", xbc_width=SSD_XBC,
                          lru=(proj_x, lru_col, lrug_col, w_blk, gb_blk, lam, l_b))
    y_ssd, y_lru = _ssd_sweep(proj_x, dt_x, alog_row, s_f, hexp, direction="fwd", mode="full", xbc_width=SSD_XBC,
                              lru=(proj_x, lru_col, lrug_col, w_blk, gb_blk, lam, l_f),
                              dskip=dskip, yin=y_b, hin=h_b)
    return _post(x, y_ssd, proj_x, y_lru, mod_x, ssd_norm_w[l], w_br_ssd[l], w_br_lru[l], w_out[l],
                 ln1_g[l], ln1_b[l], w_mlp1[l], b_mlp1[l], w_mlp2[l], b_mlp2[l], ln2_g[l], ln2_b[l],
                 tb=min(t, POST_TB), z_col=z_col, gate_col=gate_col)
```

```python
import functools

import jax
import jax.numpy as jnp
from jax import lax
from jax.experimental import pallas as pl
from jax.experimental.pallas import tpu as pltpu

F32 = jnp.float32
BF16 = jnp.bfloat16
HIGHEST = lax.Precision.HIGHEST

D_MODEL = 1024
GRID_W = 64
SSD_D_INNER = 2 * D_MODEL
SSD_HEADDIM = 64
SSD_HEADS = SSD_D_INNER // SSD_HEADDIM
SSD_GROUPS = 8
SSD_HPG = SSD_HEADS // SSD_GROUPS
SSD_STATE = 128
SSD_CHUNK = 128
SSD_GROUP_W = SSD_HPG * SSD_HEADDIM
CONV_K = 4
LRU_WIDTH = D_MODEL
LRU_BLOCKS = 8
LRU_BLOCK_W = LRU_WIDTH // LRU_BLOCKS
LRU_C = 8.0
MLP_HIDDEN = 4 * D_MODEL
N_MOD = 6
DEPTH = 1
DEEPNORM_ALPHA = (2 * DEPTH) ** 0.25
LN_EPS = 1e-6
RMS_EPS = 1e-5
LOG2E = 1.4426950408889634
assert SSD_CHUNK // 2 == SSD_HEADDIM

SSD_BC_W = SSD_GROUPS * SSD_STATE
SSD_XB = SSD_D_INNER + SSD_BC_W
SSD_XBC = SSD_D_INNER + 2 * SSD_BC_W
SSD_DT = 2 * SSD_HEADS
O_DT = SSD_XB
O_LRU = O_DT + SSD_DT
STATE_COLS = O_LRU + LRU_WIDTH
O_C = STATE_COLS
O_Z = O_C + SSD_BC_W
O_LRU_GATE = O_Z + SSD_D_INNER
O_MERGE = O_LRU_GATE + LRU_WIDTH

LANES = 128
SUBLANES = 8
VMEM_LIMIT = 58 * 1024 * 1024

PROJ_TN = 1024
PROJ_SUB = 256
PROJ_TB = 512
CONV_GAP = SUBLANES
DT_W = LANES
SSD_TB = 256
SSD_NB = 2
LRU_TC = 128
LRU_VT = 64
POST_TB = 512
POST_SUB = 256


def _silu(v):
    return v / (1.0 + jnp.exp(-v))


def _sigmoid(v):
    return 1.0 / (1.0 + jnp.exp(-v))


def _softplus(v):
    return jnp.maximum(v, 0.0) + jnp.log1p(jnp.exp(-jnp.abs(v)))


def _layer_norm(v):
    mu = jnp.mean(v, axis=-1, keepdims=True)
    vc = v - mu
    var = jnp.mean(vc * vc, axis=-1, keepdims=True)
    return vc * lax.rsqrt(var + LN_EPS)


def _group_roll(v, shift):
    rows, width = v.shape
    v3 = v.reshape(rows // SUBLANES, SUBLANES, width)
    return pltpu.roll(v3, shift, 1).reshape(rows, width)


def _split3_bf16(v):
    p1 = v.astype(BF16)
    r1 = v - p1.astype(F32)
    p2 = r1.astype(BF16)
    p3 = (r1 - p2.astype(F32)).astype(BF16)
    return p1, p2, p3


def _tri_rows(tri, v):
    w = v.shape[1]
    out = jnp.dot(tri, jnp.concatenate(_split3_bf16(v), axis=1), preferred_element_type=F32)
    return out[:, :w] + out[:, w:2 * w] + out[:, 2 * w:]


def _tri_cols(v, tri):
    h = v.shape[0]
    out = jnp.dot(jnp.concatenate(_split3_bf16(v), axis=0), tri, preferred_element_type=F32)
    return out[:h, :] + out[h:2 * h, :] + out[2 * h:, :]


def _split_bf16(v):
    hi = v.astype(BF16)
    lo = (v - hi.astype(F32)).astype(BF16)
    return hi, lo


def _mod_kernel(c_ref, w_ref, b_ref, o_ref):
    s = _silu(c_ref[...])
    o_ref[...] = jnp.dot(s, w_ref[...], precision=HIGHEST, preferred_element_type=F32) + b_ref[...]


def _modulation(cvecs, w_mod, b_mod):
    rows, ncols = cvecs.shape[0], w_mod.shape[1]
    tn = 512
    return pl.pallas_call(
        _mod_kernel,
        grid=(ncols // tn,),
        in_specs=[pl.BlockSpec((rows, D_MODEL), lambda j: (0, 0)),
                  pl.BlockSpec((D_MODEL, tn), lambda j: (0, j)),
                  pl.BlockSpec((1, tn), lambda j: (0, j))],
        out_specs=pl.BlockSpec((rows, tn), lambda j: (0, j)),
        out_shape=jax.ShapeDtypeStruct((rows, ncols), F32),
        name="modulation",
    )(cvecs, w_mod, b_mod.reshape(1, ncols))


def _inproj_kernel(x_ref, shift_ref, scale_ref, w_ref, cw_ref, cb_ref, wdt_ref, dtb_ref,
                   p_ref, dt_ref, h_scr, c_scr, *, segs, row_len, tb):
    sub = PROJ_SUB
    npiece = sub // row_len
    pitch = row_len + CONV_GAP
    zgap = jnp.zeros((CONV_GAP, PROJ_TN), F32)
    for r in range(npiece + 1):
        c_scr[r * pitch:r * pitch + CONV_GAP, :] = zgap

    def sub_block(s, carry):
        rows = pl.ds(pl.multiple_of(s * sub, sub), sub)
        h = _layer_norm(x_ref[0, rows, :]) * (1.0 + scale_ref[0]) + shift_ref[0]
        hb = h.astype(BF16)
        h_scr[...] = hb
        dt2 = jnp.dot(hb, wdt_ref[...], preferred_element_type=F32)
        dt_ref[0, rows, :] = _softplus(dt2[:, :DT_W] + dt2[:, DT_W:] + dtb_ref[...])
        for lo, hi, kind in segs:
            for j in range(lo, hi):
                cols = slice(j * PROJ_TN, (j + 1) * PROJ_TN)
                acc = jnp.dot(h_scr[...], w_ref[:, cols], preferred_element_type=F32)
                cb = cb_ref[:, cols]
                if kind in ("conv_silu", "conv"):
                    cw = cw_ref[:, cols]
                    for r in range(npiece):
                        base = CONV_GAP + r * pitch
                        c_scr[base:base + row_len, :] = acc[r * row_len:(r + 1) * row_len, :]
                    pieces = []
                    for r in range(npiece):
                        base = CONV_GAP + r * pitch
                        o = cb + acc[r * row_len:(r + 1) * row_len, :] * cw[2:3]
                        o = o + c_scr[base - 2:base - 2 + row_len, :] * cw[0:1]
                        o = o + c_scr[base - 1:base - 1 + row_len, :] * cw[1:2]
                        o = o + c_scr[base + 1:base + 1 + row_len, :] * cw[3:4]
                        pieces.append(o)
                    out = pieces[0] if npiece == 1 else jnp.concatenate(pieces, axis=0)
                    if kind == "conv_silu":
                        out = _silu(out)
                elif kind == "silu":
                    out = _silu(acc)
                elif kind == "gelu":
                    out = 0.5 * acc * (1.0 + jnp.tanh(0.7978845608028654 * (acc + 0.044715 * acc * acc * acc)))
                elif kind == "sigmoid_bias":
                    out = _sigmoid(acc + cb)
                else:
                    raise ValueError(kind)
                p_ref[0, rows, cols] = out.astype(BF16)
        return carry

    lax.fori_loop(0, tb // sub, sub_block, 0)


def _inproj(xs, shift, scale, w_all, cw_all, cb_all, w_dt2, dt_bias, *, segs, row_len, tb):
    bsz, t, _ = xs.shape
    ncols = w_all.shape[1]
    kern = functools.partial(_inproj_kernel, segs=segs, row_len=row_len, tb=tb)
    const = lambda shape: pl.BlockSpec(shape, lambda b, i: (0, 0), pipeline_mode=pl.Buffered(1))
    conv_rows = CONV_GAP + (PROJ_SUB // row_len) * (row_len + CONV_GAP)
    return pl.pallas_call(
        kern,
        grid=(bsz, t // tb),
        in_specs=[pl.BlockSpec((1, tb, D_MODEL), lambda b, i: (b, i, 0)),
                  pl.BlockSpec((1, 1, D_MODEL), lambda b, i: (b, 0, 0)),
                  pl.BlockSpec((1, 1, D_MODEL), lambda b, i: (b, 0, 0)),
                  const((D_MODEL, ncols)),
                  const((SUBLANES, ncols)),
                  const((1, ncols)),
                  const((D_MODEL, 2 * DT_W)),
                  const((1, DT_W))],
        out_specs=[pl.BlockSpec((1, tb, ncols), lambda b, i: (b, i, 0)),
                   pl.BlockSpec((1, tb, DT_W), lambda b, i: (b, i, 0))],
        out_shape=[jax.ShapeDtypeStruct((bsz, t, ncols), BF16),
                   jax.ShapeDtypeStruct((bsz, t, DT_W), F32)],
        scratch_shapes=[pltpu.VMEM((PROJ_SUB, D_MODEL), BF16),
                        pltpu.VMEM((conv_rows, PROJ_TN), F32)],
        compiler_params=pltpu.CompilerParams(
            dimension_semantics=("parallel", "parallel"),
            vmem_limit_bytes=VMEM_LIMIT),
        name="inproj",
    )(xs, shift, scale, w_all, cw_all, cb_all, w_dt2, dt_bias)


def _ssd_kernel(*refs, direction, mode, nsteps, nsub, nbatch):
    if mode == "full":
        (xbc_ref, dt_ref, alog_ref, h0_ref, hexp_ref, u_ref, lw_ref, lgb_ref, lam_ref, lh0_ref,
         dskip_ref, yin_ref, hin_ref, g_ref,
         y_ref, hout_ref, s_scr, e_scr, lc_scr, y_scr, cb_scr) = refs
    elif mode == "inter":
        (xbc_ref, dt_ref, alog_ref, h0_ref, hexp_ref, u_ref, lw_ref, lgb_ref, lam_ref, lh0_ref,
         y_ref, hout_ref, s_scr, e_scr, lc_scr) = refs
    else:
        xbc_ref, dt_ref, alog_ref, h0_ref, hexp_ref, sfin_ref, s_scr, e_scr = refs
    step = pl.program_id(1)
    L = SSD_CHUNK
    fwd = direction == "fwd"
    with_lru = mode != "state"
    groups = [slice(g * SSD_GROUP_W, (g + 1) * SSD_GROUP_W) for g in range(SSD_GROUPS)]
    bcols = [slice(SSD_D_INNER + g * SSD_STATE, SSD_D_INNER + (g + 1) * SSD_STATE) for g in range(SSD_GROUPS)]
    ccols = [slice(SSD_XB + g * SSD_STATE, SSD_XB + (g + 1) * SSD_STATE) for g in range(SSD_GROUPS)]

    @pl.when(step == 0)
    def _():
        s_scr[...] = h0_ref[...]
        if with_lru:
            for bb in range(nbatch):
                lc_scr[bb] = jnp.broadcast_to(lh0_ref[bb], (SUBLANES, LRU_WIDTH))

    if with_lru:
        ldir = 0 if fwd else 1
        chalf, oks = _lru_consts(lam_ref[ldir:ldir + 1, :], fwd)
        ldcols = slice(ldir * 2 * LRU_BLOCK_W, (ldir + 1) * 2 * LRU_BLOCK_W)

    def lru_tiles(bb, rows, ks):
        if not with_lru:
            return
        for k in ks:
            cols = slice(k * LRU_BLOCK_W, (k + 1) * LRU_BLOCK_W)
            hv, c = _lru_block(u_ref[bb, rows, cols], lw_ref[k, :, ldcols], lgb_ref[k:k + 1, ldcols],
                               chalf[:, cols], oks, lc_scr[bb, :, cols], fwd)
            lc_scr[bb, :, cols] = c
            if mode == "full":
                hv = (hv + hin_ref[bb, rows, cols].astype(F32)) * g_ref[bb, rows, cols].astype(F32)
            hout_ref[bb, rows, cols] = hv.astype(hout_ref.dtype)

    a_neg = -jnp.exp(alog_ref[...]) * LOG2E
    ri = lax.broadcasted_iota(jnp.int32, (L, L), 0)
    ci = lax.broadcasted_iota(jnp.int32, (L, L), 1)
    tril = jnp.where(ri >= ci, 1.0, 0.0).astype(BF16)
    triu = jnp.where(ri <= ci, 1.0, 0.0).astype(BF16)

    def expand(vals, col0):
        hi, lo = _split_bf16(vals[:, col0:col0 + SSD_HEADS])
        return jnp.dot(jnp.concatenate([hi, lo], axis=1), hexp_ref[...], preferred_element_type=F32)

    def chunk(i, bb):
        cidx = i if fwd else nsub - 1 - i
        rows = pl.ds(cidx * L, L)
        es = e_scr.at[bb, i]
        ss = s_scr.at[bb]
        if mode == "full":
            ys = y_scr.at[bb, i]
            cs = cb_scr.at[bb, i]
        dt = dt_ref[bb, rows, :]
        dta = dt * a_neg
        cum = _tri_rows(tril if fwd else triu, dta)
        total = cum[L - 1:L, :] if fwd else cum[0:1, :]
        w_state = dt * jnp.exp2(total - cum)
        dec_tot = jnp.broadcast_to(jnp.exp2(total), (SUBLANES, DT_W))
        if mode == "state":
            stack = jnp.concatenate([w_state, dec_tot], axis=0)
        elif mode == "inter":
            stack = jnp.concatenate([w_state, dec_tot, jnp.exp2(cum)], axis=0)
        else:
            stack = jnp.concatenate([w_state, dec_tot, cum], axis=0)
        lru_tiles(bb, rows, (0, 1))
        yield
        es[...] = expand(stack, 0 if fwd else SSD_HEADS)
        r_dec = L
        r_ein = L + SUBLANES
        lru_tiles(bb, rows, (2, 3))
        if mode == "full":
            P = SSD_HEADDIM
            half = L // 2
            cum_b = _tri_rows(triu, dta)
            cs[...] = expand(cum_b, SSD_HEADS)
            dta_t = dta.T
            ldt_t = jnp.log2(dt.T)
            row_f = ldt_t - _tri_cols(dta_t, triu)
            row_b = ldt_t - _tri_cols(dta_t, tril)
            ti = lax.broadcasted_iota(jnp.int32, (half, 2 * P), 0)
            tj = lax.broadcasted_iota(jnp.int32, (half, 2 * P), 1) % P
            lo_tri = ti >= tj
            up_tri = ti <= tj
            lane_a = lax.broadcasted_iota(jnp.int32, (half, 2 * P), 1) < P

            def intra_group(g):
                bg = xbc_ref[bb, rows, bcols[g]]
                cg = xbc_ref[bb, rows, ccols[g]]
                b4 = jnp.concatenate([bg[:half, :], bg[:half, :], bg[half:, :], bg[half:, :]], axis=0)
                cb4 = lax.dot_general(cg, b4, (((1,), (1,)), ((), ())), preferred_element_type=F32)
                for pp in range(SSD_HPG // 2):
                    ha = g * SSD_HPG + 2 * pp
                    hb = SSD_HEADS + ha
                    px = slice(ha * P, (ha + 2) * P)
                    col_f = es[r_ein:r_ein + L, px]
                    col_b = cs[:, px]
                    xp = xbc_ref[bb, rows, px]
                    ms, rhss = [], []
                    for jh in range(2):
                        js = slice(jh * half, (jh + 1) * half)
                        rf = jnp.concatenate([row_f[ha:ha + 1, js], row_f[ha + 1:ha + 2, js]], axis=1)
                        rb = jnp.concatenate([row_b[hb:hb + 1, js], row_b[hb + 1:hb + 2, js]], axis=1)
                        dg = slice(jh * half, (jh + 1) * half)
                        m_diag = (jnp.where(lo_tri, jnp.exp2(col_f[dg, :] + rf), 0.0)
                                  + jnp.where(up_tri, jnp.exp2(col_b[dg, :] + rb), 0.0))
                        if jh == 0:
                            m = jnp.concatenate([m_diag, jnp.exp2(col_f[half:, :] + rf)], axis=0)
                        else:
                            m = jnp.concatenate([jnp.exp2(col_b[:half, :] + rb), m_diag], axis=0)
                        ms.append((m * cb4[:, jh * 2 * P:(jh + 1) * 2 * P]).astype(BF16))
                        xj = xp[js, :]
                        zero = jnp.zeros_like(xj)
                        rhss += [jnp.where(lane_a, xj, zero), jnp.where(lane_a, zero, xj)]
                    y2 = jnp.dot(jnp.concatenate(ms, axis=1), jnp.concatenate(rhss, axis=0),
                                 preferred_element_type=F32)
                    y_ref[bb, rows, px] = (ys[:, px] + y2).astype(y_ref.dtype)

        yield

        if mode != "state":
            for g, gx in enumerate(groups):
                cg = xbc_ref[bb, rows, ccols[g]]
                e_in = es[r_ein:r_ein + L, gx]
                if mode == "full":
                    e_in = jnp.exp2(e_in)
                y_g = e_in * jnp.dot(cg, ss[:, gx].astype(BF16), preferred_element_type=F32)
                if mode == "inter":
                    y_ref[bb, rows, gx] = y_g.astype(y_ref.dtype)
                else:
                    xg = xbc_ref[bb, rows, gx].astype(F32)
                    ys[:, gx] = y_g + yin_ref[bb, rows, gx].astype(F32) + dskip_ref[:, gx] * xg

        lru_tiles(bb, rows, (4, 5))
        yield
        if mode == "full":
            for g in range(SSD_GROUPS // 2):
                intra_group(g)
                yield
        for g, gx in enumerate(groups):
            bg = xbc_ref[bb, rows, bcols[g]]
            xw = (xbc_ref[bb, rows, gx].astype(F32) * es[0:L, gx]).astype(BF16)
            upd = lax.dot_general(bg, xw, (((0,), (0,)), ((), ())), preferred_element_type=F32)
            ss[:, gx] = es[r_dec:r_dec + 1, gx] * ss[:, gx] + upd
        lru_tiles(bb, rows, (6, 7))
        yield

        if mode == "full":
            for g in range(SSD_GROUPS // 2, SSD_GROUPS):
                intra_group(g)
                yield

    gens = [[chunk(i, bb) for bb in range(nbatch)] for i in range(nsub)]
    for _ in range(2):
        for i in range(nsub):
            for gen in gens[i]:
                next(gen)
    for i in range(nsub):
        for _ in zip(*gens[i]):
            pass

    if mode == "state":
        @pl.when(step == nsteps - 1)
        def _():
            sfin_ref[...] = s_scr[...]


def _ssd_sweep(xbc, dt, alog_row, h0, hexp, *, direction, mode, xbc_width, lru=None, dskip=None, yin=None,
               hin=None):
    bsz, t, _ = xbc.shape
    tb = min(t, SSD_TB)
    nb = SSD_NB if bsz % SSD_NB == 0 else 1
    nsteps = t // tb
    hp = SSD_D_INNER
    w = LRU_WIDTH
    row = (lambda c: c) if direction == "fwd" else (lambda c: nsteps - 1 - c)
    cmap = lambda b, c: (b, row(c), 0)
    const = lambda shape: pl.BlockSpec(shape, lambda b, c: (0,) * len(shape))
    in_specs = [pl.BlockSpec((nb, tb, xbc_width), cmap),
                pl.BlockSpec((nb, tb, DT_W), cmap),
                const((1, DT_W)),
                pl.BlockSpec((nb, SSD_STATE, hp), lambda b, c: (b, 0, 0)),
                const((2 * SSD_HEADS, hp))]
    args = [xbc, dt, alog_row, h0, jnp.concatenate([hexp, hexp], axis=0)]
    stack_rows = (SSD_CHUNK + SUBLANES) if mode == "state" else (2 * SSD_CHUNK + SUBLANES)
    nsub = tb // SSD_CHUNK
    scratch = [pltpu.VMEM((nb, SSD_STATE, hp), F32), pltpu.VMEM((nb, nsub, stack_rows, hp), F32)]
    if mode != "state":
        proj, u_col, g_col, w_blk, gb_blk, lam, lh0 = lru
        in_specs += [pl.BlockSpec((nb, tb, w), lambda b, c: (b, row(c), u_col)),
                     const((LRU_BLOCKS, LRU_BLOCK_W, 4 * LRU_BLOCK_W)),
                     const((LRU_BLOCKS, 4 * LRU_BLOCK_W)),
                     const((2, w)),
                     pl.BlockSpec((nb, 1, w), lambda b, c: (b, 0, 0))]
        args += [proj, w_blk, gb_blk, lam, lh0]
        scratch.append(pltpu.VMEM((nb, SUBLANES, w), F32))
    if mode == "full":
        scratch += [pltpu.VMEM((nb, nsub, SSD_CHUNK, hp), F32), pltpu.VMEM((nb, nsub, SSD_CHUNK, hp), F32)]
        in_specs += [const((1, hp)), pl.BlockSpec((nb, tb, hp), cmap), pl.BlockSpec((nb, tb, w), cmap),
                     pl.BlockSpec((nb, tb, w), lambda b, c: (b, row(c), g_col))]
        args += [dskip, yin, hin, proj]
    if mode == "state":
        out_specs = pl.BlockSpec((nb, SSD_STATE, hp), lambda b, c: (b, 0, 0))
        out_shape = jax.ShapeDtypeStruct((bsz, SSD_STATE, hp), F32)
    else:
        out_specs = [pl.BlockSpec((nb, tb, hp), cmap), pl.BlockSpec((nb, tb, w), cmap)]
        out_shape = [jax.ShapeDtypeStruct((bsz, t, hp), BF16), jax.ShapeDtypeStruct((bsz, t, w), BF16)]
    kern = functools.partial(_ssd_kernel, direction=direction, mode=mode, nsteps=nsteps, nsub=nsub, nbatch=nb)
    return pl.pallas_call(
        kern,
        grid=(bsz // nb, nsteps),
        in_specs=in_specs,
        out_specs=out_specs,
        out_shape=out_shape,
        scratch_shapes=scratch,
        compiler_params=pltpu.CompilerParams(
            dimension_semantics=("parallel", "arbitrary"),
            vmem_limit_bytes=VMEM_LIMIT),
        name=f"ssd_{direction}_{mode}",
    )(*args)


def _lru_consts(lam_row, fwd):
    chalf = (-0.5 * LRU_C) * _softplus(-lam_row)
    sub_idx = lax.broadcasted_iota(jnp.int32, (LRU_VT, LRU_BLOCK_W), 0) % SUBLANES
    oks = {s: (sub_idx >= s) if fwd else (sub_idx < SUBLANES - s) for s in (1, 2, 4)}
    return chalf, oks


def _lru_block(ub, wk, gbk, chalf_k, oks, c, fwd):
    bw = LRU_BLOCK_W
    pre = jnp.dot(ub, wk, preferred_element_type=F32) + gbk
    nsl = LRU_TC // LRU_VT
    hs = [None] * nsl
    for si in (range(nsl) if fwd else range(nsl - 1, -1, -1)):
        rs = slice(si * LRU_VT, (si + 1) * LRU_VT)
        tr = jnp.tanh(pre[rs, :bw])
        ti = jnp.tanh(pre[rs, bw:])
        log_a = tr * chalf_k + chalf_k
        a = jnp.exp(log_a)
        th = jnp.tanh(log_a)
        b = jnp.sqrt(-0.5 * th / (1.0 - th)) * ((ti + 1.0) * ub[rs, :].astype(F32))
        ngrp = LRU_VT // SUBLANES
        a_t = pltpu.einshape("gsl->sgl", a.reshape(ngrp, SUBLANES, bw))
        b_t = pltpu.einshape("gsl->sgl", b.reshape(ngrp, SUBLANES, bw))
        h_loc, p_loc = [None] * SUBLANES, [None] * SUBLANES
        h = p = None
        for s in (range(SUBLANES) if fwd else range(SUBLANES - 1, -1, -1)):
            h = b_t[s] if h is None else a_t[s] * h + b_t[s]
            p = a_t[s] if p is None else a_t[s] * p
            h_loc[s], p_loc[s] = h, p
        cur = c[0:1, :]
        h_in = [None] * ngrp
        for gi in (range(ngrp) if fwd else range(ngrp - 1, -1, -1)):
            h_in[gi] = cur
            cur = p[gi:gi + 1, :] * cur + h[gi:gi + 1, :]
        c = jnp.broadcast_to(cur, (SUBLANES, bw))
        h_in = jnp.concatenate(h_in, axis=0)
        h_t = jnp.stack([h_loc[s] + p_loc[s] * h_in for s in range(SUBLANES)], axis=0)
        hs[si] = pltpu.einshape("sgl->gsl", h_t).reshape(LRU_VT, bw)
    return jnp.concatenate(hs, axis=0), c


def _lru_state_kernel(u_ref, w_ref, gb_ref, lam_ref, sf_ref, sb_ref, *, t):
    bw = LRU_BLOCK_W
    nchunk = t // LRU_TC
    for d in range(2):
        fwd = d == 0
        chalf, oks = _lru_consts(lam_ref[d:d + 1, :], fwd)
        dcols = slice(d * 2 * bw, (d + 1) * 2 * bw)

        def chunk(i, carry, fwd=fwd, chalf=chalf, oks=oks, dcols=dcols):
            cidx = i if fwd else nchunk - 1 - i
            rows = pl.ds(pl.multiple_of(cidx * LRU_TC, LRU_TC), LRU_TC)
            out = []
            for k in range(LRU_BLOCKS):
                cols = slice(k * bw, (k + 1) * bw)
                _, c = _lru_block(u_ref[0, rows, cols], w_ref[k, :, dcols], gb_ref[k:k + 1, dcols],
                                  chalf[:, cols], oks, carry[k], fwd)
                out.append(c)
            return tuple(out)

        init = tuple(jnp.zeros((SUBLANES, bw), F32) for _ in range(LRU_BLOCKS))
        fin = lax.fori_loop(0, nchunk, chunk, init)
        for k in range(LRU_BLOCKS):
            (sf_ref if fwd else sb_ref)[0, :, k * bw:(k + 1) * bw] = fin[k][0:1, :]


def _lru_state(u_src, u_col, w_blk, gb_blk, lam):
    bsz, t, _ = u_src.shape
    w = LRU_WIDTH
    return pl.pallas_call(
        functools.partial(_lru_state_kernel, t=t),
        grid=(bsz,),
        in_specs=[pl.BlockSpec((1, t, w), lambda b: (b, 0, u_col)),
                  pl.BlockSpec((LRU_BLOCKS, LRU_BLOCK_W, 4 * LRU_BLOCK_W), lambda b: (0, 0, 0)),
                  pl.BlockSpec((LRU_BLOCKS, 4 * LRU_BLOCK_W), lambda b: (0, 0)),
                  pl.BlockSpec((2, w), lambda b: (0, 0))],
        out_specs=[pl.BlockSpec((1, 1, w), lambda b: (b, 0, 0)),
                   pl.BlockSpec((1, 1, w), lambda b: (b, 0, 0))],
        out_shape=[jax.ShapeDtypeStruct((bsz, 1, w), F32),
                   jax.ShapeDtypeStruct((bsz, 1, w), F32)],
        compiler_params=pltpu.CompilerParams(
            dimension_semantics=("parallel",),
            vmem_limit_bytes=VMEM_LIMIT),
        name="lru_state",
    )(u_src, w_blk, gb_blk, lam)


def _post_kernel(x_ref, y_ref, z_ref, gate_ref, ylru_ref, mod_ref, nw_ref, wbs_ref, wbl_ref, wout_ref,
                 ln1g_ref, ln1b_ref, w1_ref, b1_ref, w2_ref, b2_ref, ln2g_ref, ln2b_ref, o_ref, *, tb):
    d = D_MODEL
    mod = mod_ref[0]
    gate1 = mod[:, 2 * d:3 * d]
    shift2 = mod[:, 3 * d:4 * d]
    scale2 = mod[:, 4 * d:5 * d]
    gate2 = mod[:, 5 * d:6 * d]

    def slab(rows):
        u = y_ref[0, rows, :].astype(F32) * z_ref[0, rows, :].astype(F32)
        parts = []
        for g in range(SSD_GROUPS):
            ug = u[:, g * SSD_GROUP_W:(g + 1) * SSD_GROUP_W]
            ms = jnp.mean(ug * ug, axis=-1, keepdims=True)
            parts.append(ug * lax.rsqrt(ms + RMS_EPS))
        un = (jnp.concatenate(parts, axis=1) * nw_ref[...]).astype(BF16)
        yield
        br_ssd = jnp.dot(un, wbs_ref[...], preferred_element_type=F32)
        br_lru = jnp.dot(ylru_ref[0, rows, :], wbl_ref[...], preferred_element_type=F32)
        gates = gate_ref[0, rows, :].astype(F32)
        merged = (gates[:, :d] * br_ssd + gates[:, d:] * br_lru).astype(BF16)
        yield
        x_mix = jnp.dot(merged, wout_ref[...], preferred_element_type=F32)
        x1 = _layer_norm(DEEPNORM_ALPHA * x_ref[0, rows, :] + gate1 * x_mix) * ln1g_ref[...] + ln1b_ref[...]
        h2 = (_layer_norm(x1) * (1.0 + scale2) + shift2).astype(BF16)
        yield
        hid = jnp.dot(h2, w1_ref[...], preferred_element_type=F32) + b1_ref[...]
        hid = jnp.maximum(hid, 0.0)
        hid = (hid * hid).astype(BF16)
        yield
        mlp = jnp.dot(hid, w2_ref[...], preferred_element_type=F32) + b2_ref[...]
        o_ref[0, rows, :] = _layer_norm(DEEPNORM_ALPHA * x1 + gate2 * mlp) * ln2g_ref[...] + ln2b_ref[...]
        yield

    gens = [slab(pl.ds(r * POST_SUB, POST_SUB)) for r in range(tb // POST_SUB)]
    nstage = 5
    for tick in range(nstage + len(gens) - 1):
        for r, gen in enumerate(gens):
            if 0 <= tick - r < nstage:
                next(gen)


def _const_spec(shape):
    nd = len(shape)
    return pl.BlockSpec(shape, lambda b, i: (0,) * nd, pipeline_mode=pl.Buffered(1))


def _post(x, y_ssd, proj, y_lru, mod, norm_w, w_br_ssd, w_br_lru, w_out, ln1_g, ln1_b,
          w1, b1, w2, b2, ln2_g, ln2_b, *, tb, z_col, gate_col):
    bsz, t, d = x.shape
    di = SSD_D_INNER
    row = lambda v: v.reshape(1, -1)
    return pl.pallas_call(
        functools.partial(_post_kernel, tb=tb),
        grid=(bsz, t // tb),
        in_specs=[pl.BlockSpec((1, tb, d), lambda b, i: (b, i, 0)),
                  pl.BlockSpec((1, tb, di), lambda b, i: (b, i, 0)),
                  pl.BlockSpec((1, tb, di), lambda b, i: (b, i, z_col)),
                  pl.BlockSpec((1, tb, di), lambda b, i: (b, i, gate_col)),
                  pl.BlockSpec((1, tb, d), lambda b, i: (b, i, 0)),
                  pl.BlockSpec((1, 1, N_MOD * d), lambda b, i: (b, 0, 0)),
                  _const_spec((1, di)),
                  _const_spec((di, d)),
                  _const_spec((d, d)),
                  _const_spec((d, d)),
                  _const_spec((1, d)),
                  _const_spec((1, d)),
                  _const_spec((d, MLP_HIDDEN)),
                  _const_spec((1, MLP_HIDDEN)),
                  _const_spec((MLP_HIDDEN, d)),
                  _const_spec((1, d)),
                  _const_spec((1, d)),
                  _const_spec((1, d))],
        out_specs=pl.BlockSpec((1, tb, d), lambda b, i: (b, i, 0)),
        out_shape=jax.ShapeDtypeStruct((bsz, t, d), F32),
        compiler_params=pltpu.CompilerParams(
            dimension_semantics=("parallel", "parallel"),
            vmem_limit_bytes=VMEM_LIMIT),
        name="post",
    )(x, y_ssd, proj, proj, y_lru, mod, row(norm_w), w_br_ssd.astype(BF16), w_br_lru.astype(BF16),
      w_out.astype(BF16), row(ln1_g), row(ln1_b), w1.astype(BF16), row(b1), w2.astype(BF16), row(b2),
      row(ln2_g), row(ln2_b))


def _pad_rows(v, n):
    return jnp.pad(v, ((0, n - v.shape[0]), (0, 0)))


def kernel(x, c, ctx, c_ctx, w_mod, b_mod, w_in, b_gate, ssd_conv_w, ssd_conv_b, ssd_dt_bias, ssd_a_log,
           ssd_d, ssd_norm_w, lru_conv_w, lru_conv_b, lru_wa, lru_ba, lru_wi, lru_bi, lru_lambda, w_br_ssd,
           w_br_lru, w_out, ln1_g, ln1_b, w_mlp1, b_mlp1, w_mlp2, b_mlp2, ln2_g, ln2_b):
    bsz, t, d = x.shape
    tctx = ctx.shape[1]
    l = 0
    wi = w_in[l]

    nrow = -(-(bsz + 1) // SUBLANES) * SUBLANES
    cvecs = _pad_rows(jnp.concatenate([c, c_ctx[None, :]], axis=0), nrow)
    mods = _modulation(cvecs, w_mod[l], b_mod[l])
    mod_x = mods[:bsz].reshape(bsz, 1, N_MOD * d)
    shift_x, scale_x = mod_x[:, :, :d], mod_x[:, :, d:2 * d]
    shift_c = jnp.broadcast_to(mods[bsz:bsz + 1, :d].reshape(1, 1, d), (bsz, 1, d))
    scale_c = jnp.broadcast_to(mods[bsz:bsz + 1, d:2 * d].reshape(1, 1, d), (bsz, 1, d))

    zc = lambda n: jnp.zeros((CONV_K, n), F32)
    z1 = lambda n: jnp.zeros((n,), F32)
    w_x = jnp.concatenate([wi[:, :SSD_XB], wi[:, O_C:O_Z], wi[:, O_Z:O_LRU_GATE], wi[:, O_MERGE:],
                           wi[:, O_LRU:STATE_COLS], wi[:, O_LRU_GATE:O_MERGE]], axis=1).astype(BF16)
    cw_x = _pad_rows(jnp.concatenate([ssd_conv_w[l], zc(SSD_D_INNER), zc(2 * d), lru_conv_w[l], zc(LRU_WIDTH)],
                                     axis=1), SUBLANES)
    cb_x = jnp.concatenate([ssd_conv_b[l], z1(SSD_D_INNER), b_gate[l], lru_conv_b[l], z1(LRU_WIDTH)])[None, :]
    n_xbc, n_z, n_m, n_l = SSD_XBC // PROJ_TN, SSD_D_INNER // PROJ_TN, 2 * d // PROJ_TN, LRU_WIDTH // PROJ_TN
    segs_x, lo = [], 0
    for n, kind in ((n_xbc, "conv_silu"), (n_z, "silu"), (n_m, "sigmoid_bias"), (n_l, "conv"), (n_l, "gelu")):
        segs_x.append((lo, lo + n, kind))
        lo += n
    z_col = SSD_XBC // SSD_D_INNER
    gate_col = z_col + 1
    lru_col = (SSD_XBC + 2 * SSD_D_INNER) // LRU_WIDTH
    lrug_col = lru_col + 1

    w_c = jnp.concatenate([wi[:, :SSD_XB], wi[:, O_LRU:STATE_COLS]], axis=1).astype(BF16)
    cw_c = _pad_rows(jnp.concatenate([ssd_conv_w[l][:, :SSD_XB], lru_conv_w[l]], axis=1), SUBLANES)
    cb_c = jnp.concatenate([ssd_conv_b[l][:SSD_XB], lru_conv_b[l]])[None, :]
    segs_c = [(0, SSD_XB // PROJ_TN, "conv_silu"), (SSD_XB // PROJ_TN, SSD_XB // PROJ_TN + n_l, "conv")]
    lru_col_c = SSD_XB // LRU_WIDTH

    w_dt = jnp.pad(wi[:, O_DT:O_LRU], ((0, 0), (0, DT_W - SSD_DT)))
    w_dt_hi = w_dt.astype(BF16)
    w_dt_lo = (w_dt - w_dt_hi.astype(F32)).astype(BF16)
    w_dt2 = jnp.concatenate([w_dt_hi, w_dt_lo], axis=1)
    dt_bias = jnp.pad(ssd_dt_bias[l].reshape(1, SSD_DT), ((0, 0), (0, DT_W - SSD_DT)))
    alog_row = jnp.pad(ssd_a_log[l].reshape(1, SSD_DT), ((0, 0), (0, DT_W - SSD_DT)))
    dskip = jnp.repeat(ssd_d[l], SSD_HEADDIM)[None, :]
    hexp = jnp.repeat(jnp.eye(SSD_HEADS, dtype=BF16), SSD_HEADDIM, axis=1)

    w_blk = (0.5 * jnp.concatenate([lru_wa[l, 0], lru_wi[l, 0], lru_wa[l, 1], lru_wi[l, 1]], axis=-1)).astype(BF16)
    blk = lambda v: v.reshape(LRU_BLOCKS, LRU_BLOCK_W)
    gb_blk = 0.5 * jnp.concatenate([blk(lru_ba[l, 0]), blk(lru_bi[l, 0]), blk(lru_ba[l, 1]), blk(lru_bi[l, 1])],
                                   axis=-1)
    lam = lru_lambda[l]

    proj_c, dt_c = _inproj(ctx, shift_c, scale_c, w_c, cw_c, cb_c, w_dt2, dt_bias,
                           segs=segs_c, row_len=tctx, tb=tctx)
    zero_s = jnp.zeros((bsz, SSD_STATE, SSD_D_INNER), F32)
    s_f = _ssd_sweep(proj_c, dt_c, alog_row, zero_s, hexp, direction="fwd", mode="state", xbc_width=SSD_XB)
    s_b = _ssd_sweep(proj_c, dt_c, alog_row, zero_s, hexp, direction="bwd", mode="state", xbc_width=SSD_XB)
    l_f, l_b = _lru_state(proj_c, lru_col_c, w_blk, gb_blk, lam)

    proj_x, dt_x = _inproj(x, shift_x, scale_x, w_x, cw_x, cb_x, w_dt2, dt_bias,
                           segs=segs_x, row_len=GRID_W, tb=min(t, PROJ_TB))
    y_b, h_b = _ssd_sweep(proj_x, dt_x, alog_row, s_b, hexp, direction="bwd", mode="inter", xbc_width=SSD_XBC,
                          lru=(proj_x, lru_col, lrug_col, w_blk, gb_blk, lam, l_b))
    y_ssd, y_lru = _ssd_sweep(proj_x, dt_x, alog_row, s_f, hexp, direction="fwd", mode="full", xbc_width=SSD_XBC,
                              lru=(proj_x, lru_col, lrug_col, w_blk, gb_blk, lam, l_f),
                              dskip=dskip, yin=y_b, hin=h_b)
    return _post(x, y_ssd, proj_x, y_lru, mod_x, ssd_norm_w[l], w_br_ssd[l], w_br_lru[l], w_out[l],
                 ln1_g[l], ln1_b[l], w_mlp1[l], b_mlp1[l], w_mlp2[l], b_mlp2[l], ln2_g[l], ln2_b[l],
                 tb=min(t, POST_TB), z_col=z_col, gate_col=gate_col)
```

```python
import functools

import jax
import jax.numpy as jnp
from jax import lax
from jax.experimental import pallas as pl
from jax.experimental.pallas import tpu as pltpu

F32 = jnp.float32
BF16 = jnp.bfloat16
HIGHEST = lax.Precision.HIGHEST

D_MODEL = 1024
GRID_W = 64
SSD_D_INNER = 2 * D_MODEL
SSD_HEADDIM = 64
SSD_HEADS = SSD_D_INNER // SSD_HEADDIM
SSD_GROUPS = 8
SSD_HPG = SSD_HEADS // SSD_GROUPS
SSD_STATE = 128
SSD_CHUNK = 128
SSD_GROUP_W = SSD_HPG * SSD_HEADDIM
CONV_K = 4
LRU_WIDTH = D_MODEL
LRU_BLOCKS = 8
LRU_BLOCK_W = LRU_WIDTH // LRU_BLOCKS
LRU_C = 8.0
MLP_HIDDEN = 4 * D_MODEL
N_MOD = 6
DEPTH = 1
DEEPNORM_ALPHA = (2 * DEPTH) ** 0.25
LN_EPS = 1e-6
RMS_EPS = 1e-5
LOG2E = 1.4426950408889634
assert SSD_CHUNK // 2 == SSD_HEADDIM

SSD_BC_W = SSD_GROUPS * SSD_STATE
SSD_XB = SSD_D_INNER + SSD_BC_W
SSD_XBC = SSD_D_INNER + 2 * SSD_BC_W
SSD_DT = 2 * SSD_HEADS
O_DT = SSD_XB
O_LRU = O_DT + SSD_DT
STATE_COLS = O_LRU + LRU_WIDTH
O_C = STATE_COLS
O_Z = O_C + SSD_BC_W
O_LRU_GATE = O_Z + SSD_D_INNER
O_MERGE = O_LRU_GATE + LRU_WIDTH

LANES = 128
SUBLANES = 8
VMEM_LIMIT = 58 * 1024 * 1024

PROJ_TN = 1024
PROJ_SUB = 256
PROJ_TB = 512
CONV_GAP = SUBLANES
DT_W = LANES
SSD_TB = 256
SSD_NB = 2
LRU_TC = 128
LRU_VT = 64
POST_TB = 512
POST_SUB = 256


def _silu(v):
    return v / (1.0 + jnp.exp(-v))


def _sigmoid(v):
    return 1.0 / (1.0 + jnp.exp(-v))


def _softplus(v):
    return jnp.maximum(v, 0.0) + jnp.log1p(jnp.exp(-jnp.abs(v)))


def _layer_norm(v):
    mu = jnp.mean(v, axis=-1, keepdims=True)
    vc = v - mu
    var = jnp.mean(vc * vc, axis=-1, keepdims=True)
    return vc * lax.rsqrt(var + LN_EPS)


def _split3_bf16(v):
    p1 = v.astype(BF16)
    r1 = v - p1.astype(F32)
    p2 = r1.astype(BF16)
    p3 = (r1 - p2.astype(F32)).astype(BF16)
    return p1, p2, p3


def _tri_rows(tri, v):
    w = v.shape[1]
    out = jnp.dot(tri, jnp.concatenate(_split3_bf16(v), axis=1), preferred_element_type=F32)
    return out[:, :w] + out[:, w:2 * w] + out[:, 2 * w:]


def _tri_cols(v, tri):
    h = v.shape[0]
    out = jnp.dot(jnp.concatenate(_split3_bf16(v), axis=0), tri, preferred_element_type=F32)
    return out[:h, :] + out[h:2 * h, :] + out[2 * h:, :]


def _mod_kernel(c_ref, w_ref, b_ref, o_ref):
    s = _silu(c_ref[...])
    o_ref[...] = jnp.dot(s, w_ref[...], precision=HIGHEST, preferred_element_type=F32) + b_ref[...]


def _modulation(cvecs, w_mod, b_mod):
    rows, ncols = cvecs.shape[0], w_mod.shape[1]
    tn = 512
    return pl.pallas_call(
        _mod_kernel,
        grid=(ncols // tn,),
        in_specs=[pl.BlockSpec((rows, D_MODEL), lambda j: (0, 0)),
                  pl.BlockSpec((D_MODEL, tn), lambda j: (0, j)),
                  pl.BlockSpec((1, tn), lambda j: (0, j))],
        out_specs=pl.BlockSpec((rows, tn), lambda j: (0, j)),
        out_shape=jax.ShapeDtypeStruct((rows, ncols), F32),
        name="modulation",
    )(cvecs, w_mod, b_mod.reshape(1, ncols))


def _inproj_kernel(x_ref, shift_ref, scale_ref, w_ref, cw_ref, cb_ref, wdt_ref, dtb_ref,
                   p_ref, dt_ref, h_scr, c_scr, *, segs, row_len, tb):
    sub = PROJ_SUB
    npiece = sub // row_len
    pitch = row_len + CONV_GAP
    zgap = jnp.zeros((CONV_GAP, PROJ_TN), F32)
    for r in range(npiece + 1):
        c_scr[r * pitch:r * pitch + CONV_GAP, :] = zgap

    def sub_block(s, carry):
        rows = pl.ds(pl.multiple_of(s * sub, sub), sub)
        h = _layer_norm(x_ref[0, rows, :]) * (1.0 + scale_ref[0]) + shift_ref[0]
        hb = h.astype(BF16)
        h_scr[...] = hb
        dt2 = jnp.dot(hb, wdt_ref[...], preferred_element_type=F32)
        dt_ref[0, rows, :] = _softplus(dt2[:, :DT_W] + dt2[:, DT_W:] + dtb_ref[...])
        for lo, hi, kind in segs:
            for j in range(lo, hi):
                cols = slice(j * PROJ_TN, (j + 1) * PROJ_TN)
                acc = jnp.dot(h_scr[...], w_ref[:, cols], preferred_element_type=F32)
                cb = cb_ref[:, cols]
                if kind in ("conv_silu", "conv"):
                    cw = cw_ref[:, cols]
                    for r in range(npiece):
                        base = CONV_GAP + r * pitch
                        c_scr[base:base + row_len, :] = acc[r * row_len:(r + 1) * row_len, :]
                    pieces = []
                    for r in range(npiece):
                        base = CONV_GAP + r * pitch
                        o = cb + acc[r * row_len:(r + 1) * row_len, :] * cw[2:3]
                        o = o + c_scr[base - 2:base - 2 + row_len, :] * cw[0:1]
                        o = o + c_scr[base - 1:base - 1 + row_len, :] * cw[1:2]
                        o = o + c_scr[base + 1:base + 1 + row_len, :] * cw[3:4]
                        pieces.append(o)
                    out = pieces[0] if npiece == 1 else jnp.concatenate(pieces, axis=0)
                    if kind == "conv_silu":
                        out = _silu(out)
                elif kind == "silu":
                    out = _silu(acc)
                elif kind == "gelu":
                    out = 0.5 * acc * (1.0 + jnp.tanh(0.7978845608028654 * (acc + 0.044715 * acc * acc * acc)))
                elif kind == "sigmoid_bias":
                    out = _sigmoid(acc + cb)
                else:
                    raise ValueError(kind)
                p_ref[0, rows, cols] = out.astype(BF16)
        return carry

    lax.fori_loop(0, tb // sub, sub_block, 0)


def _inproj(xs, shift, scale, w_all, cw_all, cb_all, w_dt2, dt_bias, *, segs, row_len, tb):
    bsz, t, _ = xs.shape
    ncols = w_all.shape[1]
    kern = functools.partial(_inproj_kernel, segs=segs, row_len=row_len, tb=tb)
    const = lambda shape: pl.BlockSpec(shape, lambda b, i: (0, 0), pipeline_mode=pl.Buffered(1))
    conv_rows = CONV_GAP + (PROJ_SUB // row_len) * (row_len + CONV_GAP)
    return pl.pallas_call(
        kern,
        grid=(bsz, t // tb),
        in_specs=[pl.BlockSpec((1, tb, D_MODEL), lambda b, i: (b, i, 0)),
                  pl.BlockSpec((1, 1, D_MODEL), lambda b, i: (b, 0, 0)),
                  pl.BlockSpec((1, 1, D_MODEL), lambda b, i: (b, 0, 0)),
                  const((D_MODEL, ncols)),
                  const((SUBLANES, ncols)),
                  const((1, ncols)),
                  const((D_MODEL, 2 * DT_W)),
                  const((1, DT_W))],
        out_specs=[pl.BlockSpec((1, tb, ncols), lambda b, i: (b, i, 0)),
                   pl.BlockSpec((1, tb, DT_W), lambda b, i: (b, i, 0))],
        out_shape=[jax.ShapeDtypeStruct((bsz, t, ncols), BF16),
                   jax.ShapeDtypeStruct((bsz, t, DT_W), F32)],
        scratch_shapes=[pltpu.VMEM((PROJ_SUB, D_MODEL), BF16),
                        pltpu.VMEM((conv_rows, PROJ_TN), F32)],
        compiler_params=pltpu.CompilerParams(
            dimension_semantics=("parallel", "parallel"),
            vmem_limit_bytes=VMEM_LIMIT),
        name="inproj",
    )(xs, shift, scale, w_all, cw_all, cb_all, w_dt2, dt_bias)


def _ssd_kernel(*refs, direction, mode, nsteps, nsub, nbatch):
    if mode == "full":
        (xbc_ref, dt_ref, alog_ref, h0_ref, hexp_ref, u_ref, lw_ref, lgb_ref, lam_ref, lh0_ref,
         dskip_ref, yin_ref, hin_ref, g_ref,
         y_ref, hout_ref, s_scr, e_scr, lc_scr, y_scr, cb_scr) = refs
    elif mode == "inter":
        (xbc_ref, dt_ref, alog_ref, h0_ref, hexp_ref, u_ref, lw_ref, lgb_ref, lam_ref, lh0_ref,
         y_ref, hout_ref, s_scr, e_scr, lc_scr) = refs
    else:
        xbc_ref, dt_ref, alog_ref, hexp_ref, sfin_ref, s_scr, e_scr = refs
    step = pl.program_id(1)
    L = SSD_CHUNK
    fwd = direction == "fwd"
    with_lru = mode != "state"
    groups = [slice(g * SSD_GROUP_W, (g + 1) * SSD_GROUP_W) for g in range(SSD_GROUPS)]
    bcols = [slice(SSD_D_INNER + g * SSD_STATE, SSD_D_INNER + (g + 1) * SSD_STATE) for g in range(SSD_GROUPS)]
    ccols = [slice(SSD_XB + g * SSD_STATE, SSD_XB + (g + 1) * SSD_STATE) for g in range(SSD_GROUPS)]

    @pl.when(step == 0)
    def _():
        s_scr[...] = jnp.zeros(s_scr.shape, F32) if mode == "state" else h0_ref[...]
        if with_lru:
            for bb in range(nbatch):
                lc_scr[bb] = jnp.broadcast_to(lh0_ref[bb], (SUBLANES, LRU_WIDTH))

    if with_lru:
        ldir = 0 if fwd else 1
        chalf = _lru_half_log_a(lam_ref[ldir:ldir + 1, :])
        ldcols = slice(ldir * 2 * LRU_BLOCK_W, (ldir + 1) * 2 * LRU_BLOCK_W)

    def lru_tiles(bb, rows, ks):
        if not with_lru:
            return
        for k in ks:
            cols = slice(k * LRU_BLOCK_W, (k + 1) * LRU_BLOCK_W)
            hv, c = _lru_block(u_ref[bb, rows, cols], lw_ref[k, :, ldcols], lgb_ref[k:k + 1, ldcols],
                               chalf[:, cols], lc_scr[bb, :, cols], fwd)
            lc_scr[bb, :, cols] = c
            if mode == "full":
                hv = (hv + hin_ref[bb, rows, cols].astype(F32)) * g_ref[bb, rows, cols].astype(F32)
            hout_ref[bb, rows, cols] = hv.astype(hout_ref.dtype)

    a_neg = -jnp.exp(alog_ref[...]) * LOG2E
    ri = lax.broadcasted_iota(jnp.int32, (L, L), 0)
    ci = lax.broadcasted_iota(jnp.int32, (L, L), 1)
    tril = jnp.where(ri >= ci, 1.0, 0.0).astype(BF16)
    triu = jnp.where(ri <= ci, 1.0, 0.0).astype(BF16)

    def expand(vals, col0):
        pieces = _split3_bf16(vals[:, col0:col0 + SSD_HEADS])
        return jnp.dot(jnp.concatenate(pieces, axis=1), hexp_ref[...], preferred_element_type=F32)

    def chunk(i, bb):
        cidx = i if fwd else nsub - 1 - i
        rows = pl.ds(cidx * L, L)
        es = e_scr.at[bb, i]
        ss = s_scr.at[bb]
        if mode == "full":
            ys = y_scr.at[bb, i]
            cs = cb_scr.at[bb, i]
        dt = dt_ref[bb, rows, :]
        dta = dt * a_neg
        cum = _tri_rows(tril if fwd else triu, dta)
        total = cum[L - 1:L, :] if fwd else cum[0:1, :]
        w_state = dt * jnp.exp2(total - cum)
        dec_tot = jnp.broadcast_to(jnp.exp2(total), (SUBLANES, DT_W))
        if mode == "state":
            stack = jnp.concatenate([w_state, dec_tot], axis=0)
        elif mode == "inter":
            stack = jnp.concatenate([w_state, dec_tot, jnp.exp2(cum)], axis=0)
        else:
            stack = jnp.concatenate([w_state, dec_tot, cum], axis=0)
        lru_tiles(bb, rows, (0, 1))
        yield
        es[...] = expand(stack, 0 if fwd else SSD_HEADS)
        r_dec = L
        r_ein = L + SUBLANES
        lru_tiles(bb, rows, (2, 3))
        if mode == "full":
            P = SSD_HEADDIM
            half = L // 2
            cum_b = _tri_rows(triu, dta)
            cs[...] = expand(cum_b, SSD_HEADS)
            dta_t = dta.T
            ldt_t = jnp.log2(dt.T)
            row_f = ldt_t - _tri_cols(dta_t, triu)
            row_b = ldt_t - _tri_cols(dta_t, tril)
            ti = lax.broadcasted_iota(jnp.int32, (half, 2 * P), 0)
            tj = lax.broadcasted_iota(jnp.int32, (half, 2 * P), 1) % P
            lo_tri = ti >= tj
            up_tri = ti <= tj
            lane_a = lax.broadcasted_iota(jnp.int32, (half, 2 * P), 1) < P

            def intra_group(g):
                bg = xbc_ref[bb, rows, bcols[g]]
                cg = xbc_ref[bb, rows, ccols[g]]
                b4 = jnp.concatenate([bg[:half, :], bg[:half, :], bg[half:, :], bg[half:, :]], axis=0)
                cb4 = lax.dot_general(cg, b4, (((1,), (1,)), ((), ())), preferred_element_type=F32)
                for pp in range(SSD_HPG // 2):
                    ha = g * SSD_HPG + 2 * pp
                    hb = SSD_HEADS + ha
                    px = slice(ha * P, (ha + 2) * P)
                    col_f = es[r_ein:r_ein + L, px]
                    col_b = cs[:, px]
                    xp = xbc_ref[bb, rows, px]
                    ms, rhss = [], []
                    for jh in range(2):
                        js = slice(jh * half, (jh + 1) * half)
                        rf = jnp.concatenate([row_f[ha:ha + 1, js], row_f[ha + 1:ha + 2, js]], axis=1)
                        rb = jnp.concatenate([row_b[hb:hb + 1, js], row_b[hb + 1:hb + 2, js]], axis=1)
                        dg = slice(jh * half, (jh + 1) * half)
                        zb = jnp.zeros((half, 2 * P), BF16)
                        m_diag = (jnp.where(lo_tri, jnp.exp2(col_f[dg, :] + rf).astype(BF16), zb)
                                  + jnp.where(up_tri, jnp.exp2(col_b[dg, :] + rb).astype(BF16), zb))
                        if jh == 0:
                            m = jnp.concatenate([m_diag, jnp.exp2(col_f[half:, :] + rf).astype(BF16)], axis=0)
                        else:
                            m = jnp.concatenate([jnp.exp2(col_b[:half, :] + rb).astype(BF16), m_diag], axis=0)
                        ms.append(m * cb4[:, jh * 2 * P:(jh + 1) * 2 * P].astype(BF16))
                        xj = xp[js, :]
                        zero = jnp.zeros_like(xj)
                        rhss += [jnp.where(lane_a, xj, zero), jnp.where(lane_a, zero, xj)]
                    y2 = jnp.dot(jnp.concatenate(ms, axis=1), jnp.concatenate(rhss, axis=0),
                                 preferred_element_type=F32)
                    y_ref[bb, rows, px] = (ys[:, px] + y2).astype(y_ref.dtype)

        yield

        if mode != "state":
            for g, gx in enumerate(groups):
                cg = xbc_ref[bb, rows, ccols[g]]
                e_in = es[r_ein:r_ein + L, gx]
                if mode == "full":
                    e_in = jnp.exp2(e_in)
                y_g = e_in * jnp.dot(cg, ss[:, gx].astype(BF16), preferred_element_type=F32)
                if mode == "inter":
                    y_ref[bb, rows, gx] = y_g.astype(y_ref.dtype)
                else:
                    xg = xbc_ref[bb, rows, gx].astype(F32)
                    ys[:, gx] = y_g + yin_ref[bb, rows, gx].astype(F32) + dskip_ref[:, gx] * xg

        lru_tiles(bb, rows, (4, 5))
        yield
        if mode == "full":
            for g in range(SSD_GROUPS // 2):
                intra_group(g)
                yield
        for g, gx in enumerate(groups):
            bg = xbc_ref[bb, rows, bcols[g]]
            xw = xbc_ref[bb, rows, gx] * es[0:L, gx].astype(BF16)
            upd = lax.dot_general(bg, xw, (((0,), (0,)), ((), ())), preferred_element_type=F32)
            ss[:, gx] = es[r_dec:r_dec + 1, gx] * ss[:, gx] + upd
        lru_tiles(bb, rows, (6, 7))
        yield

        if mode == "full":
            for g in range(SSD_GROUPS // 2, SSD_GROUPS):
                intra_group(g)
                yield

    gens = [[chunk(i, bb) for bb in range(nbatch)] for i in range(nsub)]
    for _ in range(2):
        for i in range(nsub):
            for gen in gens[i]:
                next(gen)
    for i in range(nsub):
        for _ in zip(*gens[i]):
            pass

    if mode == "state":
        @pl.when(step == nsteps - 1)
        def _():
            sfin_ref[...] = s_scr[...]


def _ssd_sweep(xbc, dt, alog_row, h0, hexp, *, direction, mode, xbc_width, lru=None, dskip=None, yin=None,
               hin=None):
    bsz, t, _ = xbc.shape
    tb = min(t, SSD_TB)
    nb = SSD_NB if bsz % SSD_NB == 0 else 1
    nsteps = t // tb
    hp = SSD_D_INNER
    w = LRU_WIDTH
    row = (lambda c: c) if direction == "fwd" else (lambda c: nsteps - 1 - c)
    cmap = lambda b, c: (b, row(c), 0)
    const = lambda shape: pl.BlockSpec(shape, lambda b, c: (0,) * len(shape))
    in_specs = [pl.BlockSpec((nb, tb, xbc_width), cmap),
                pl.BlockSpec((nb, tb, DT_W), cmap),
                const((1, DT_W))]
    args = [xbc, dt, alog_row]
    if mode != "state":
        in_specs.append(pl.BlockSpec((nb, SSD_STATE, hp), lambda b, c: (b, 0, 0)))
        args.append(h0)
    in_specs.append(const((3 * SSD_HEADS, hp)))
    args.append(jnp.concatenate([hexp, hexp, hexp], axis=0))
    stack_rows = (SSD_CHUNK + SUBLANES) if mode == "state" else (2 * SSD_CHUNK + SUBLANES)
    nsub = tb // SSD_CHUNK
    scratch = [pltpu.VMEM((nb, SSD_STATE, hp), F32), pltpu.VMEM((nb, nsub, stack_rows, hp), F32)]
    if mode != "state":
        proj, u_col, g_col, w_blk, gb_blk, lam, lh0 = lru
        in_specs += [pl.BlockSpec((nb, tb, w), lambda b, c: (b, row(c), u_col)),
                     const((LRU_BLOCKS, LRU_BLOCK_W, 4 * LRU_BLOCK_W)),
                     const((LRU_BLOCKS, 4 * LRU_BLOCK_W)),
                     const((2, w)),
                     pl.BlockSpec((nb, 1, w), lambda b, c: (b, 0, 0))]
        args += [proj, w_blk, gb_blk, lam, lh0]
        scratch.append(pltpu.VMEM((nb, SUBLANES, w), F32))
    if mode == "full":
        scratch += [pltpu.VMEM((nb, nsub, SSD_CHUNK, hp), F32), pltpu.VMEM((nb, nsub, SSD_CHUNK, hp), F32)]
        in_specs += [const((1, hp)), pl.BlockSpec((nb, tb, hp), cmap), pl.BlockSpec((nb, tb, w), cmap),
                     pl.BlockSpec((nb, tb, w), lambda b, c: (b, row(c), g_col))]
        args += [dskip, yin, hin, proj]
    if mode == "state":
        out_specs = pl.BlockSpec((nb, SSD_STATE, hp), lambda b, c: (b, 0, 0))
        out_shape = jax.ShapeDtypeStruct((bsz, SSD_STATE, hp), F32)
    else:
        out_specs = [pl.BlockSpec((nb, tb, hp), cmap), pl.BlockSpec((nb, tb, w), cmap)]
        out_shape = [jax.ShapeDtypeStruct((bsz, t, hp), BF16), jax.ShapeDtypeStruct((bsz, t, w), BF16)]
    kern = functools.partial(_ssd_kernel, direction=direction, mode=mode, nsteps=nsteps, nsub=nsub, nbatch=nb)
    return pl.pallas_call(
        kern,
        grid=(bsz // nb, nsteps),
        in_specs=in_specs,
        out_specs=out_specs,
        out_shape=out_shape,
        scratch_shapes=scratch,
        compiler_params=pltpu.CompilerParams(
            dimension_semantics=("parallel", "arbitrary"),
            vmem_limit_bytes=VMEM_LIMIT),
        name=f"ssd_{direction}_{mode}",
    )(*args)


def _lru_half_log_a(lam_row):
    return (-0.5 * LRU_C) * _softplus(-lam_row)


def _lru_block(ub, wk, gbk, chalf_k, c, fwd):
    bw = LRU_BLOCK_W
    pre = jnp.dot(ub, wk, preferred_element_type=F32) + gbk
    nsl = LRU_TC // LRU_VT
    hs = [None] * nsl
    for si in (range(nsl) if fwd else range(nsl - 1, -1, -1)):
        rs = slice(si * LRU_VT, (si + 1) * LRU_VT)
        tr = jnp.tanh(pre[rs, :bw])
        ti = jnp.tanh(pre[rs, bw:])
        log_a = tr * chalf_k + chalf_k
        a = jnp.exp(log_a)
        th = jnp.tanh(log_a)
        b = jnp.sqrt(-0.5 * th / (1.0 - th)) * ((ti + 1.0) * ub[rs, :].astype(F32))
        ngrp = LRU_VT // SUBLANES
        a_t = pltpu.einshape("gsl->sgl", a.reshape(ngrp, SUBLANES, bw))
        b_t = pltpu.einshape("gsl->sgl", b.reshape(ngrp, SUBLANES, bw))
        h_loc, p_loc = [None] * SUBLANES, [None] * SUBLANES
        h = p = None
        for s in (range(SUBLANES) if fwd else range(SUBLANES - 1, -1, -1)):
            h = b_t[s] if h is None else a_t[s] * h + b_t[s]
            p = a_t[s] if p is None else a_t[s] * p
            h_loc[s], p_loc[s] = h, p
        cur = c[0:1, :]
        h_in = [None] * ngrp
        for gi in (range(ngrp) if fwd else range(ngrp - 1, -1, -1)):
            h_in[gi] = cur
            cur = p[gi:gi + 1, :] * cur + h[gi:gi + 1, :]
        c = jnp.broadcast_to(cur, (SUBLANES, bw))
        h_in = jnp.concatenate(h_in, axis=0)
        h_t = jnp.stack([h_loc[s] + p_loc[s] * h_in for s in range(SUBLANES)], axis=0)
        hs[si] = pltpu.einshape("sgl->gsl", h_t).reshape(LRU_VT, bw)
    return jnp.concatenate(hs, axis=0), c


def _lru_state_kernel(u_ref, w_ref, gb_ref, lam_ref, sf_ref, sb_ref, *, t):
    bw = LRU_BLOCK_W
    nchunk = t // LRU_TC
    for d in range(2):
        fwd = d == 0
        chalf = _lru_half_log_a(lam_ref[d:d + 1, :])
        dcols = slice(d * 2 * bw, (d + 1) * 2 * bw)

        def chunk(i, carry, fwd=fwd, chalf=chalf, dcols=dcols):
            cidx = i if fwd else nchunk - 1 - i
            rows = pl.ds(pl.multiple_of(cidx * LRU_TC, LRU_TC), LRU_TC)
            out = []
            for k in range(LRU_BLOCKS):
                cols = slice(k * bw, (k + 1) * bw)
                _, c = _lru_block(u_ref[0, rows, cols], w_ref[k, :, dcols], gb_ref[k:k + 1, dcols],
                                  chalf[:, cols], carry[k], fwd)
                out.append(c)
            return tuple(out)

        init = tuple(jnp.zeros((SUBLANES, bw), F32) for _ in range(LRU_BLOCKS))
        fin = lax.fori_loop(0, nchunk, chunk, init)
        for k in range(LRU_BLOCKS):
            (sf_ref if fwd else sb_ref)[0, :, k * bw:(k + 1) * bw] = fin[k][0:1, :]


def _lru_state(u_src, u_col, w_blk, gb_blk, lam):
    bsz, t, _ = u_src.shape
    w = LRU_WIDTH
    return pl.pallas_call(
        functools.partial(_lru_state_kernel, t=t),
        grid=(bsz,),
        in_specs=[pl.BlockSpec((1, t, w), lambda b: (b, 0, u_col)),
                  pl.BlockSpec((LRU_BLOCKS, LRU_BLOCK_W, 4 * LRU_BLOCK_W), lambda b: (0, 0, 0)),
                  pl.BlockSpec((LRU_BLOCKS, 4 * LRU_BLOCK_W), lambda b: (0, 0)),
                  pl.BlockSpec((2, w), lambda b: (0, 0))],
        out_specs=[pl.BlockSpec((1, 1, w), lambda b: (b, 0, 0)),
                   pl.BlockSpec((1, 1, w), lambda b: (b, 0, 0))],
        out_shape=[jax.ShapeDtypeStruct((bsz, 1, w), F32),
                   jax.ShapeDtypeStruct((bsz, 1, w), F32)],
        compiler_params=pltpu.CompilerParams(
            dimension_semantics=("parallel",),
            vmem_limit_bytes=VMEM_LIMIT),
        name="lru_state",
    )(u_src, w_blk, gb_blk, lam)


def _post_kernel(x_ref, y_ref, z_ref, gate_ref, ylru_ref, mod_ref, nw_ref, wbs_ref, wbl_ref, wout_ref,
                 ln1g_ref, ln1b_ref, w1_ref, b1_ref, w2_ref, b2_ref, ln2g_ref, ln2b_ref, o_ref, *, tb):
    d = D_MODEL
    mod = mod_ref[0]
    gate1 = mod[:, 2 * d:3 * d]
    shift2 = mod[:, 3 * d:4 * d]
    scale2 = mod[:, 4 * d:5 * d]
    gate2 = mod[:, 5 * d:6 * d]

    def slab(rows):
        u = y_ref[0, rows, :].astype(F32) * z_ref[0, rows, :].astype(F32)
        parts = []
        for g in range(SSD_GROUPS):
            ug = u[:, g * SSD_GROUP_W:(g + 1) * SSD_GROUP_W]
            ms = jnp.mean(ug * ug, axis=-1, keepdims=True)
            parts.append(ug * lax.rsqrt(ms + RMS_EPS))
        un = (jnp.concatenate(parts, axis=1) * nw_ref[...]).astype(BF16)
        yield
        br_ssd = jnp.dot(un, wbs_ref[...], preferred_element_type=F32)
        br_lru = jnp.dot(ylru_ref[0, rows, :], wbl_ref[...], preferred_element_type=F32)
        gates = gate_ref[0, rows, :].astype(F32)
        merged = (gates[:, :d] * br_ssd + gates[:, d:] * br_lru).astype(BF16)
        yield
        x_mix = jnp.dot(merged, wout_ref[...], preferred_element_type=F32)
        x1 = _layer_norm(DEEPNORM_ALPHA * x_ref[0, rows, :] + gate1 * x_mix) * ln1g_ref[...] + ln1b_ref[...]
        h2 = (_layer_norm(x1) * (1.0 + scale2) + shift2).astype(BF16)
        yield
        hid = jnp.dot(h2, w1_ref[...], preferred_element_type=F32) + b1_ref[...]
        hid = jnp.maximum(hid, 0.0)
        hid = (hid * hid).astype(BF16)
        yield
        mlp = jnp.dot(hid, w2_ref[...], preferred_element_type=F32) + b2_ref[...]
        o_ref[0, rows, :] = _layer_norm(DEEPNORM_ALPHA * x1 + gate2 * mlp) * ln2g_ref[...] + ln2b_ref[...]
        yield

    gens = [slab(pl.ds(r * POST_SUB, POST_SUB)) for r in range(tb // POST_SUB)]
    nstage = 5
    for tick in range(nstage + len(gens) - 1):
        for r, gen in enumerate(gens):
            if 0 <= tick - r < nstage:
                next(gen)


def _const_spec(shape):
    nd = len(shape)
    return pl.BlockSpec(shape, lambda b, i: (0,) * nd, pipeline_mode=pl.Buffered(1))


def _post(x, y_ssd, proj, y_lru, mod, norm_w, w_br_ssd, w_br_lru, w_out, ln1_g, ln1_b,
          w1, b1, w2, b2, ln2_g, ln2_b, *, tb, z_col, gate_col):
    bsz, t, d = x.shape
    di = SSD_D_INNER
    row = lambda v: v.reshape(1, -1)
    return pl.pallas_call(
        functools.partial(_post_kernel, tb=tb),
        grid=(bsz, t // tb),
        in_specs=[pl.BlockSpec((1, tb, d), lambda b, i: (b, i, 0)),
                  pl.BlockSpec((1, tb, di), lambda b, i: (b, i, 0)),
                  pl.BlockSpec((1, tb, di), lambda b, i: (b, i, z_col)),
                  pl.BlockSpec((1, tb, di), lambda b, i: (b, i, gate_col)),
                  pl.BlockSpec((1, tb, d), lambda b, i: (b, i, 0)),
                  pl.BlockSpec((1, 1, N_MOD * d), lambda b, i: (b, 0, 0)),
                  _const_spec((1, di)),
                  _const_spec((di, d)),
                  _const_spec((d, d)),
                  _const_spec((d, d)),
                  _const_spec((1, d)),
                  _const_spec((1, d)),
                  _const_spec((d, MLP_HIDDEN)),
                  _const_spec((1, MLP_HIDDEN)),
                  _const_spec((MLP_HIDDEN, d)),
                  _const_spec((1, d)),
                  _const_spec((1, d)),
                  _const_spec((1, d))],
        out_specs=pl.BlockSpec((1, tb, d), lambda b, i: (b, i, 0)),
        out_shape=jax.ShapeDtypeStruct((bsz, t, d), F32),
        compiler_params=pltpu.CompilerParams(
            dimension_semantics=("parallel", "parallel"),
            vmem_limit_bytes=VMEM_LIMIT),
        name="post",
    )(x, y_ssd, proj, proj, y_lru, mod, row(norm_w), w_br_ssd.astype(BF16), w_br_lru.astype(BF16),
      w_out.astype(BF16), row(ln1_g), row(ln1_b), w1.astype(BF16), row(b1), w2.astype(BF16), row(b2),
      row(ln2_g), row(ln2_b))


def _pad_rows(v, n):
    return jnp.pad(v, ((0, n - v.shape[0]), (0, 0)))


def kernel(x, c, ctx, c_ctx, w_mod, b_mod, w_in, b_gate, ssd_conv_w, ssd_conv_b, ssd_dt_bias, ssd_a_log,
           ssd_d, ssd_norm_w, lru_conv_w, lru_conv_b, lru_wa, lru_ba, lru_wi, lru_bi, lru_lambda, w_br_ssd,
           w_br_lru, w_out, ln1_g, ln1_b, w_mlp1, b_mlp1, w_mlp2, b_mlp2, ln2_g, ln2_b):
    bsz, t, d = x.shape
    tctx = ctx.shape[1]
    l = 0
    wi = w_in[l]

    nrow = -(-(bsz + 1) // SUBLANES) * SUBLANES
    cvecs = _pad_rows(jnp.concatenate([c, c_ctx[None, :]], axis=0), nrow)
    mods = _modulation(cvecs, w_mod[l], b_mod[l])
    mod_x = mods[:bsz].reshape(bsz, 1, N_MOD * d)
    shift_x, scale_x = mod_x[:, :, :d], mod_x[:, :, d:2 * d]
    shift_c = jnp.broadcast_to(mods[bsz:bsz + 1, :d].reshape(1, 1, d), (bsz, 1, d))
    scale_c = jnp.broadcast_to(mods[bsz:bsz + 1, d:2 * d].reshape(1, 1, d), (bsz, 1, d))

    zc = lambda n: jnp.zeros((CONV_K, n), F32)
    z1 = lambda n: jnp.zeros((n,), F32)
    w_x = jnp.concatenate([wi[:, :SSD_XB], wi[:, O_C:O_Z], wi[:, O_Z:O_LRU_GATE], wi[:, O_MERGE:],
                           wi[:, O_LRU:STATE_COLS], wi[:, O_LRU_GATE:O_MERGE]], axis=1).astype(BF16)
    cw_x = _pad_rows(jnp.concatenate([ssd_conv_w[l], zc(SSD_D_INNER), zc(2 * d), lru_conv_w[l], zc(LRU_WIDTH)],
                                     axis=1), SUBLANES)
    cb_x = jnp.concatenate([ssd_conv_b[l], z1(SSD_D_INNER), b_gate[l], lru_conv_b[l], z1(LRU_WIDTH)])[None, :]
    n_xbc, n_z, n_m, n_l = SSD_XBC // PROJ_TN, SSD_D_INNER // PROJ_TN, 2 * d // PROJ_TN, LRU_WIDTH // PROJ_TN
    segs_x, lo = [], 0
    for n, kind in ((n_xbc, "conv_silu"), (n_z, "silu"), (n_m, "sigmoid_bias"), (n_l, "conv"), (n_l, "gelu")):
        segs_x.append((lo, lo + n, kind))
        lo += n
    z_col = SSD_XBC // SSD_D_INNER
    gate_col = z_col + 1
    lru_col = (SSD_XBC + 2 * SSD_D_INNER) // LRU_WIDTH
    lrug_col = lru_col + 1

    w_c = jnp.concatenate([wi[:, :SSD_XB], wi[:, O_LRU:STATE_COLS]], axis=1).astype(BF16)
    cw_c = _pad_rows(jnp.concatenate([ssd_conv_w[l][:, :SSD_XB], lru_conv_w[l]], axis=1), SUBLANES)
    cb_c = jnp.concatenate([ssd_conv_b[l][:SSD_XB], lru_conv_b[l]])[None, :]
    segs_c = [(0, SSD_XB // PROJ_TN, "conv_silu"), (SSD_XB // PROJ_TN, SSD_XB // PROJ_TN + n_l, "conv")]
    lru_col_c = SSD_XB // LRU_WIDTH

    w_dt = jnp.pad(wi[:, O_DT:O_LRU], ((0, 0), (0, DT_W - SSD_DT)))
    w_dt_hi = w_dt.astype(BF16)
    w_dt_lo = (w_dt - w_dt_hi.astype(F32)).astype(BF16)
    w_dt2 = jnp.concatenate([w_dt_hi, w_dt_lo], axis=1)
    dt_bias = jnp.pad(ssd_dt_bias[l].reshape(1, SSD_DT), ((0, 0), (0, DT_W - SSD_DT)))
    alog_row = jnp.pad(ssd_a_log[l].reshape(1, SSD_DT), ((0, 0), (0, DT_W - SSD_DT)))
    dskip = jnp.repeat(ssd_d[l], SSD_HEADDIM)[None, :]
    hexp = jnp.repeat(jnp.eye(SSD_HEADS, dtype=BF16), SSD_HEADDIM, axis=1)

    w_blk = (0.5 * jnp.concatenate([lru_wa[l, 0], lru_wi[l, 0], lru_wa[l, 1], lru_wi[l, 1]], axis=-1)).astype(BF16)
    blk = lambda v: v.reshape(LRU_BLOCKS, LRU_BLOCK_W)
    gb_blk = 0.5 * jnp.concatenate([blk(lru_ba[l, 0]), blk(lru_bi[l, 0]), blk(lru_ba[l, 1]), blk(lru_bi[l, 1])],
                                   axis=-1)
    lam = lru_lambda[l]

    proj_c, dt_c = _inproj(ctx, shift_c, scale_c, w_c, cw_c, cb_c, w_dt2, dt_bias,
                           segs=segs_c, row_len=tctx, tb=tctx)
    s_f = _ssd_sweep(proj_c, dt_c, alog_row, None, hexp, direction="fwd", mode="state", xbc_width=SSD_XB)
    s_b = _ssd_sweep(proj_c, dt_c, alog_row, None, hexp, direction="bwd", mode="state", xbc_width=SSD_XB)
    l_f, l_b = _lru_state(proj_c, lru_col_c, w_blk, gb_blk, lam)

    proj_x, dt_x = _inproj(x, shift_x, scale_x, w_x, cw_x, cb_x, w_dt2, dt_bias,
                           segs=segs_x, row_len=GRID_W, tb=min(t, PROJ_TB))
    y_b, h_b = _ssd_sweep(proj_x, dt_x, alog_row, s_b, hexp, direction="bwd", mode="inter", xbc_width=SSD_XBC,
                          lru=(proj_x, lru_col, lrug_col, w_blk, gb_blk, lam, l_b))
    y_ssd, y_lru = _ssd_sweep(proj_x, dt_x, alog_row, s_f, hexp, direction="fwd", mode="full", xbc_width=SSD_XBC,
                              lru=(proj_x, lru_col, lrug_col, w_blk, gb_blk, lam, l_f),
                              dskip=dskip, yin=y_b, hin=h_b)
    return _post(x, y_ssd, proj_x, y_lru, mod_x, ssd_norm_w[l], w_br_ssd[l], w_br_lru[l], w_out[l],
                 ln1_g[l], ln1_b[l], w_mlp1[l], b_mlp1[l], w_mlp2[l], b_mlp2[l], ln2_g[l], ln2_b[l],
                 tb=min(t, POST_TB), z_col=z_col, gate_col=gate_col)
```

```python
import functools

import jax
import jax.numpy as jnp
from jax import lax
from jax.experimental import pallas as pl
from jax.experimental.pallas import tpu as pltpu

F32 = jnp.float32
BF16 = jnp.bfloat16
HIGHEST = lax.Precision.HIGHEST

D_MODEL = 1024
GRID_W = 64
SSD_D_INNER = 2 * D_MODEL
SSD_HEADDIM = 64
SSD_HEADS = SSD_D_INNER // SSD_HEADDIM
SSD_GROUPS = 8
SSD_HPG = SSD_HEADS // SSD_GROUPS
SSD_STATE = 128
SSD_CHUNK = 128
SSD_GROUP_W = SSD_HPG * SSD_HEADDIM
CONV_K = 4
LRU_WIDTH = D_MODEL
LRU_BLOCKS = 8
LRU_BLOCK_W = LRU_WIDTH // LRU_BLOCKS
LRU_C = 8.0
MLP_HIDDEN = 4 * D_MODEL
N_MOD = 6
DEPTH = 1
DEEPNORM_ALPHA = (2 * DEPTH) ** 0.25
LN_EPS = 1e-6
RMS_EPS = 1e-5
LOG2E = 1.4426950408889634
assert SSD_CHUNK // 2 == SSD_HEADDIM

SSD_BC_W = SSD_GROUPS * SSD_STATE
SSD_XB = SSD_D_INNER + SSD_BC_W
SSD_XBC = SSD_D_INNER + 2 * SSD_BC_W
SSD_DT = 2 * SSD_HEADS
O_DT = SSD_XB
O_LRU = O_DT + SSD_DT
STATE_COLS = O_LRU + LRU_WIDTH
O_C = STATE_COLS
O_Z = O_C + SSD_BC_W
O_LRU_GATE = O_Z + SSD_D_INNER
O_MERGE = O_LRU_GATE + LRU_WIDTH

LANES = 128
SUBLANES = 8
VMEM_LIMIT = 58 * 1024 * 1024

PROJ_TN = 1024
PROJ_SUB = 256
PROJ_TB = 512
CONV_GAP = SUBLANES
DT_W = LANES
SSD_TB = 256
SSD_NB = 2
LRU_TC = 128
LRU_VT = 64
POST_TB = 512
POST_SUB = 256


def _silu(v):
    return v / (1.0 + jnp.exp(-v))


def _sigmoid(v):
    return 1.0 / (1.0 + jnp.exp(-v))


def _softplus(v):
    return jnp.maximum(v, 0.0) + jnp.log1p(jnp.exp(-jnp.abs(v)))


def _layer_norm(v):
    mu = jnp.mean(v, axis=-1, keepdims=True)
    vc = v - mu
    var = jnp.mean(vc * vc, axis=-1, keepdims=True)
    return vc * lax.rsqrt(var + LN_EPS)


def _split3_bf16(v):
    p1 = v.astype(BF16)
    r1 = v - p1.astype(F32)
    p2 = r1.astype(BF16)
    p3 = (r1 - p2.astype(F32)).astype(BF16)
    return p1, p2, p3


def _tri_rows(tri, v):
    w = v.shape[1]
    out = jnp.dot(tri, jnp.concatenate(_split3_bf16(v), axis=1), preferred_element_type=F32)
    return out[:, :w] + out[:, w:2 * w] + out[:, 2 * w:]


def _tri_cols(v, tri):
    h = v.shape[0]
    out = jnp.dot(jnp.concatenate(_split3_bf16(v), axis=0), tri, preferred_element_type=F32)
    return out[:h, :] + out[h:2 * h, :] + out[2 * h:, :]


def _mod_kernel(c_ref, w_ref, b_ref, o_ref):
    s = _silu(c_ref[...])
    o_ref[...] = jnp.dot(s, w_ref[...], precision=HIGHEST, preferred_element_type=F32) + b_ref[...]


def _modulation(cvecs, w_mod, b_mod):
    rows, ncols = cvecs.shape[0], w_mod.shape[1]
    tn = 512
    return pl.pallas_call(
        _mod_kernel,
        grid=(ncols // tn,),
        in_specs=[pl.BlockSpec((rows, D_MODEL), lambda j: (0, 0)),
                  pl.BlockSpec((D_MODEL, tn), lambda j: (0, j)),
                  pl.BlockSpec((1, tn), lambda j: (0, j))],
        out_specs=pl.BlockSpec((rows, tn), lambda j: (0, j)),
        out_shape=jax.ShapeDtypeStruct((rows, ncols), F32),
        name="modulation",
    )(cvecs, w_mod, b_mod.reshape(1, ncols))


def _inproj_kernel(x_ref, shift_ref, scale_ref, w_ref, cw_ref, cb_ref, wdt_ref, dtb_ref,
                   p_ref, dt_ref, h_scr, c_scr, *, segs, row_len, tb):
    sub = PROJ_SUB
    npiece = sub // row_len
    pitch = row_len + CONV_GAP
    zgap = jnp.zeros((CONV_GAP, PROJ_TN), F32)
    for r in range(npiece + 1):
        c_scr[r * pitch:r * pitch + CONV_GAP, :] = zgap

    def sub_block(s, carry):
        rows = pl.ds(pl.multiple_of(s * sub, sub), sub)
        h = _layer_norm(x_ref[0, rows, :]) * (1.0 + scale_ref[0]) + shift_ref[0]
        hb = h.astype(BF16)
        h_scr[...] = hb
        dt2 = jnp.dot(hb, wdt_ref[...], preferred_element_type=F32)
        dt_ref[0, rows, :] = _softplus(dt2[:, :DT_W] + dt2[:, DT_W:] + dtb_ref[...])
        for lo, hi, kind in segs:
            for j in range(lo, hi):
                cols = slice(j * PROJ_TN, (j + 1) * PROJ_TN)
                acc = jnp.dot(h_scr[...], w_ref[:, cols], preferred_element_type=F32)
                cb = cb_ref[:, cols]
                if kind in ("conv_silu", "conv"):
                    cw = cw_ref[:, cols]
                    for r in range(npiece):
                        base = CONV_GAP + r * pitch
                        c_scr[base:base + row_len, :] = acc[r * row_len:(r + 1) * row_len, :]
                    pieces = []
                    for r in range(npiece):
                        base = CONV_GAP + r * pitch
                        o = cb + acc[r * row_len:(r + 1) * row_len, :] * cw[2:3]
                        o = o + c_scr[base - 2:base - 2 + row_len, :] * cw[0:1]
                        o = o + c_scr[base - 1:base - 1 + row_len, :] * cw[1:2]
                        o = o + c_scr[base + 1:base + 1 + row_len, :] * cw[3:4]
                        pieces.append(o)
                    out = pieces[0] if npiece == 1 else jnp.concatenate(pieces, axis=0)
                    if kind == "conv_silu":
                        out = _silu(out)
                elif kind == "silu":
                    out = _silu(acc)
                elif kind == "gelu":
                    out = 0.5 * acc * (1.0 + jnp.tanh(0.7978845608028654 * (acc + 0.044715 * acc * acc * acc)))
                elif kind == "sigmoid_bias":
                    out = _sigmoid(acc + cb)
                else:
                    raise ValueError(kind)
                p_ref[0, rows, cols] = out.astype(BF16)
        return carry

    lax.fori_loop(0, tb // sub, sub_block, 0)


def _inproj(xs, shift, scale, w_all, cw_all, cb_all, w_dt2, dt_bias, *, segs, row_len, tb):
    bsz, t, _ = xs.shape
    ncols = w_all.shape[1]
    kern = functools.partial(_inproj_kernel, segs=segs, row_len=row_len, tb=tb)
    const = lambda shape: pl.BlockSpec(shape, lambda b, i: (0, 0), pipeline_mode=pl.Buffered(1))
    conv_rows = CONV_GAP + (PROJ_SUB // row_len) * (row_len + CONV_GAP)
    return pl.pallas_call(
        kern,
        grid=(bsz, t // tb),
        in_specs=[pl.BlockSpec((1, tb, D_MODEL), lambda b, i: (b, i, 0)),
                  pl.BlockSpec((1, 1, D_MODEL), lambda b, i: (b, 0, 0)),
                  pl.BlockSpec((1, 1, D_MODEL), lambda b, i: (b, 0, 0)),
                  const((D_MODEL, ncols)),
                  const((SUBLANES, ncols)),
                  const((1, ncols)),
                  const((D_MODEL, 2 * DT_W)),
                  const((1, DT_W))],
        out_specs=[pl.BlockSpec((1, tb, ncols), lambda b, i: (b, i, 0)),
                   pl.BlockSpec((1, tb, DT_W), lambda b, i: (b, i, 0))],
        out_shape=[jax.ShapeDtypeStruct((bsz, t, ncols), BF16),
                   jax.ShapeDtypeStruct((bsz, t, DT_W), F32)],
        scratch_shapes=[pltpu.VMEM((PROJ_SUB, D_MODEL), BF16),
                        pltpu.VMEM((conv_rows, PROJ_TN), F32)],
        compiler_params=pltpu.CompilerParams(
            dimension_semantics=("parallel", "parallel"),
            vmem_limit_bytes=VMEM_LIMIT),
        name="inproj",
    )(xs, shift, scale, w_all, cw_all, cb_all, w_dt2, dt_bias)


def _ssd_kernel(*refs, direction, mode, nsteps, nsub, nbatch):
    if mode == "full":
        (xbc_ref, dt_ref, alog_ref, h0_ref, hexp_ref, u_ref, lw_ref, lgb_ref, lam_ref, lh0_ref,
         dskip_ref, yin_ref, hin_ref, g_ref,
         y_ref, hout_ref, s_scr, e_scr, lc_scr, y_scr, cb_scr) = refs
    elif mode == "inter":
        (xbc_ref, dt_ref, alog_ref, h0_ref, hexp_ref, u_ref, lw_ref, lgb_ref, lam_ref, lh0_ref,
         y_ref, hout_ref, s_scr, e_scr, lc_scr) = refs
    else:
        (xbc_ref, dt_ref, alog_ref, hexp_ref, u_ref, lw_ref, lgb_ref, lam_ref,
         sfin_ref, lfin_ref, s_scr, e_scr, lc_scr) = refs
    step = pl.program_id(1)
    L = SSD_CHUNK
    fwd = direction == "fwd"
    groups = [slice(g * SSD_GROUP_W, (g + 1) * SSD_GROUP_W) for g in range(SSD_GROUPS)]
    bcols = [slice(SSD_D_INNER + g * SSD_STATE, SSD_D_INNER + (g + 1) * SSD_STATE) for g in range(SSD_GROUPS)]
    ccols = [slice(SSD_XB + g * SSD_STATE, SSD_XB + (g + 1) * SSD_STATE) for g in range(SSD_GROUPS)]

    @pl.when(step == 0)
    def _():
        if mode == "state":
            s_scr[...] = jnp.zeros(s_scr.shape, F32)
            lc_scr[...] = jnp.zeros(lc_scr.shape, F32)
        else:
            s_scr[...] = h0_ref[...]
            for bb in range(nbatch):
                lc_scr[bb] = jnp.broadcast_to(lh0_ref[bb], (SUBLANES, LRU_WIDTH))

    ldir = 0 if fwd else 1
    chalf = _lru_half_log_a(lam_ref[ldir:ldir + 1, :])
    ldcols = slice(ldir * 2 * LRU_BLOCK_W, (ldir + 1) * 2 * LRU_BLOCK_W)

    def lru_tiles(bb, rows, ks):
        for k in ks:
            cols = slice(k * LRU_BLOCK_W, (k + 1) * LRU_BLOCK_W)
            hv, c = _lru_block(u_ref[bb, rows, cols], lw_ref[k, :, ldcols], lgb_ref[k:k + 1, ldcols],
                               chalf[:, cols], lc_scr[bb, :, cols], fwd)
            lc_scr[bb, :, cols] = c
            if mode == "state":
                continue
            if mode == "full":
                hv = (hv + hin_ref[bb, rows, cols].astype(F32)) * g_ref[bb, rows, cols].astype(F32)
            hout_ref[bb, rows, cols] = hv.astype(hout_ref.dtype)

    a_neg = -jnp.exp(alog_ref[...]) * LOG2E
    ri = lax.broadcasted_iota(jnp.int32, (L, L), 0)
    ci = lax.broadcasted_iota(jnp.int32, (L, L), 1)
    tril = jnp.where(ri >= ci, 1.0, 0.0).astype(BF16)
    triu = jnp.where(ri <= ci, 1.0, 0.0).astype(BF16)

    def expand(vals, col0):
        pieces = _split3_bf16(vals[:, col0:col0 + SSD_HEADS])
        return jnp.dot(jnp.concatenate(pieces, axis=1), hexp_ref[...], preferred_element_type=F32)

    def chunk(i, bb):
        cidx = i if fwd else nsub - 1 - i
        rows = pl.ds(cidx * L, L)
        es = e_scr.at[bb, i]
        ss = s_scr.at[bb]
        if mode == "full":
            ys = y_scr.at[bb, i]
            cs = cb_scr.at[bb, i]
        dt = dt_ref[bb, rows, :]
        dta = dt * a_neg
        cum = _tri_rows(tril if fwd else triu, dta)
        total = cum[L - 1:L, :] if fwd else cum[0:1, :]
        w_state = dt * jnp.exp2(total - cum)
        dec_tot = jnp.broadcast_to(jnp.exp2(total), (SUBLANES, DT_W))
        if mode == "state":
            stack = jnp.concatenate([w_state, dec_tot], axis=0)
        elif mode == "inter":
            stack = jnp.concatenate([w_state, dec_tot, jnp.exp2(cum)], axis=0)
        else:
            stack = jnp.concatenate([w_state, dec_tot, cum], axis=0)
        lru_tiles(bb, rows, (0, 1))
        yield
        es[...] = expand(stack, 0 if fwd else SSD_HEADS)
        r_dec = L
        r_ein = L + SUBLANES
        lru_tiles(bb, rows, (2, 3))
        if mode == "full":
            P = SSD_HEADDIM
            half = L // 2
            cum_b = _tri_rows(triu, dta)
            cs[...] = expand(cum_b, SSD_HEADS)
            dta_t = dta.T
            ldt_t = jnp.log2(dt.T)
            row_f = ldt_t - _tri_cols(dta_t, triu)
            row_b = ldt_t - _tri_cols(dta_t, tril)
            ti = lax.broadcasted_iota(jnp.int32, (half, 2 * P), 0)
            tj = lax.broadcasted_iota(jnp.int32, (half, 2 * P), 1) % P
            lo_tri = ti >= tj
            up_tri = ti <= tj
            lane_a = lax.broadcasted_iota(jnp.int32, (half, 2 * P), 1) < P

            def intra_group(g):
                bg = xbc_ref[bb, rows, bcols[g]]
                cg = xbc_ref[bb, rows, ccols[g]]
                b4 = jnp.concatenate([bg[:half, :], bg[:half, :], bg[half:, :], bg[half:, :]], axis=0)
                cb4 = lax.dot_general(cg, b4, (((1,), (1,)), ((), ())), preferred_element_type=F32)
                for pp in range(SSD_HPG // 2):
                    ha = g * SSD_HPG + 2 * pp
                    hb = SSD_HEADS + ha
                    px = slice(ha * P, (ha + 2) * P)
                    col_f = es[r_ein:r_ein + L, px]
                    col_b = cs[:, px]
                    xp = xbc_ref[bb, rows, px]
                    ms, rhss = [], []
                    for jh in range(2):
                        js = slice(jh * half, (jh + 1) * half)
                        rf = jnp.concatenate([row_f[ha:ha + 1, js], row_f[ha + 1:ha + 2, js]], axis=1)
                        rb = jnp.concatenate([row_b[hb:hb + 1, js], row_b[hb + 1:hb + 2, js]], axis=1)
                        dg = slice(jh * half, (jh + 1) * half)
                        zb = jnp.zeros((half, 2 * P), BF16)
                        m_diag = (jnp.where(lo_tri, jnp.exp2(col_f[dg, :] + rf).astype(BF16), zb)
                                  + jnp.where(up_tri, jnp.exp2(col_b[dg, :] + rb).astype(BF16), zb))
                        if jh == 0:
                            m = jnp.concatenate([m_diag, jnp.exp2(col_f[half:, :] + rf).astype(BF16)], axis=0)
                        else:
                            m = jnp.concatenate([jnp.exp2(col_b[:half, :] + rb).astype(BF16), m_diag], axis=0)
                        ms.append(m * cb4[:, jh * 2 * P:(jh + 1) * 2 * P].astype(BF16))
                        xj = xp[js, :]
                        zero = jnp.zeros_like(xj)
                        rhss += [jnp.where(lane_a, xj, zero), jnp.where(lane_a, zero, xj)]
                    y2 = jnp.dot(jnp.concatenate(ms, axis=1), jnp.concatenate(rhss, axis=0),
                                 preferred_element_type=F32)
                    y_ref[bb, rows, px] = (ys[:, px] + y2).astype(y_ref.dtype)

        yield

        if mode != "state":
            for g, gx in enumerate(groups):
                cg = xbc_ref[bb, rows, ccols[g]]
                e_in = es[r_ein:r_ein + L, gx]
                if mode == "full":
                    e_in = jnp.exp2(e_in)
                y_g = e_in * jnp.dot(cg, ss[:, gx].astype(BF16), preferred_element_type=F32)
                if mode == "inter":
                    y_ref[bb, rows, gx] = y_g.astype(y_ref.dtype)
                else:
                    xg = xbc_ref[bb, rows, gx].astype(F32)
                    ys[:, gx] = y_g + yin_ref[bb, rows, gx].astype(F32) + dskip_ref[:, gx] * xg

        lru_tiles(bb, rows, (4, 5))
        yield
        if mode == "full":
            for g in range(SSD_GROUPS // 2):
                intra_group(g)
                yield
        for g, gx in enumerate(groups):
            bg = xbc_ref[bb, rows, bcols[g]]
            xw = xbc_ref[bb, rows, gx] * es[0:L, gx].astype(BF16)
            upd = lax.dot_general(bg, xw, (((0,), (0,)), ((), ())), preferred_element_type=F32)
            ss[:, gx] = es[r_dec:r_dec + 1, gx] * ss[:, gx] + upd
        lru_tiles(bb, rows, (6, 7))
        yield

        if mode == "full":
            for g in range(SSD_GROUPS // 2, SSD_GROUPS):
                intra_group(g)
                yield

    gens = [[chunk(i, bb) for bb in range(nbatch)] for i in range(nsub)]
    for _ in range(2):
        for i in range(nsub):
            for gen in gens[i]:
                next(gen)
    for i in range(nsub):
        for _ in zip(*gens[i]):
            pass

    if mode == "state":
        @pl.when(step == nsteps - 1)
        def _():
            sfin_ref[...] = s_scr[...]
            for bb in range(nbatch):
                lfin_ref[bb] = lc_scr[bb, 0:1, :]


def _ssd_sweep(xbc, dt, alog_row, h0, hexp, *, direction, mode, xbc_width, lru=None, dskip=None, yin=None,
               hin=None):
    bsz, t, _ = xbc.shape
    tb = min(t, SSD_TB)
    nb = SSD_NB if bsz % SSD_NB == 0 else 1
    nsteps = t // tb
    hp = SSD_D_INNER
    w = LRU_WIDTH
    row = (lambda c: c) if direction == "fwd" else (lambda c: nsteps - 1 - c)
    cmap = lambda b, c: (b, row(c), 0)
    const = lambda shape: pl.BlockSpec(shape, lambda b, c: (0,) * len(shape))
    in_specs = [pl.BlockSpec((nb, tb, xbc_width), cmap),
                pl.BlockSpec((nb, tb, DT_W), cmap),
                const((1, DT_W))]
    args = [xbc, dt, alog_row]
    if mode != "state":
        in_specs.append(pl.BlockSpec((nb, SSD_STATE, hp), lambda b, c: (b, 0, 0)))
        args.append(h0)
    in_specs.append(const((3 * SSD_HEADS, hp)))
    args.append(jnp.concatenate([hexp, hexp, hexp], axis=0))
    stack_rows = (SSD_CHUNK + SUBLANES) if mode == "state" else (2 * SSD_CHUNK + SUBLANES)
    nsub = tb // SSD_CHUNK
    scratch = [pltpu.VMEM((nb, SSD_STATE, hp), F32), pltpu.VMEM((nb, nsub, stack_rows, hp), F32)]
    proj, u_col, g_col, w_blk, gb_blk, lam, lh0 = lru
    in_specs += [pl.BlockSpec((nb, tb, w), lambda b, c: (b, row(c), u_col)),
                 const((LRU_BLOCKS, LRU_BLOCK_W, 4 * LRU_BLOCK_W)),
                 const((LRU_BLOCKS, 4 * LRU_BLOCK_W)),
                 const((2, w))]
    args += [proj, w_blk, gb_blk, lam]
    if mode != "state":
        in_specs.append(pl.BlockSpec((nb, 1, w), lambda b, c: (b, 0, 0)))
        args.append(lh0)
    scratch.append(pltpu.VMEM((nb, SUBLANES, w), F32))
    if mode == "full":
        scratch += [pltpu.VMEM((nb, nsub, SSD_CHUNK, hp), F32), pltpu.VMEM((nb, nsub, SSD_CHUNK, hp), F32)]
        in_specs += [const((1, hp)), pl.BlockSpec((nb, tb, hp), cmap), pl.BlockSpec((nb, tb, w), cmap),
                     pl.BlockSpec((nb, tb, w), lambda b, c: (b, row(c), g_col))]
        args += [dskip, yin, hin, proj]
    if mode == "state":
        out_specs = [pl.BlockSpec((nb, SSD_STATE, hp), lambda b, c: (b, 0, 0)),
                     pl.BlockSpec((nb, 1, w), lambda b, c: (b, 0, 0))]
        out_shape = [jax.ShapeDtypeStruct((bsz, SSD_STATE, hp), F32), jax.ShapeDtypeStruct((bsz, 1, w), F32)]
    else:
        out_specs = [pl.BlockSpec((nb, tb, hp), cmap), pl.BlockSpec((nb, tb, w), cmap)]
        out_shape = [jax.ShapeDtypeStruct((bsz, t, hp), BF16), jax.ShapeDtypeStruct((bsz, t, w), BF16)]
    kern = functools.partial(_ssd_kernel, direction=direction, mode=mode, nsteps=nsteps, nsub=nsub, nbatch=nb)
    return pl.pallas_call(
        kern,
        grid=(bsz // nb, nsteps),
        in_specs=in_specs,
        out_specs=out_specs,
        out_shape=out_shape,
        scratch_shapes=scratch,
        compiler_params=pltpu.CompilerParams(
            dimension_semantics=("parallel", "arbitrary"),
            vmem_limit_bytes=VMEM_LIMIT),
        name=f"ssd_{direction}_{mode}",
    )(*args)


def _lru_half_log_a(lam_row):
    return (-0.5 * LRU_C) * _softplus(-lam_row)


def _lru_block(ub, wk, gbk, chalf_k, c, fwd):
    bw = LRU_BLOCK_W
    pre = jnp.dot(ub, wk, preferred_element_type=F32) + gbk
    nsl = LRU_TC // LRU_VT
    hs = [None] * nsl
    for si in (range(nsl) if fwd else range(nsl - 1, -1, -1)):
        rs = slice(si * LRU_VT, (si + 1) * LRU_VT)
        tr = jnp.tanh(pre[rs, :bw])
        ti = jnp.tanh(pre[rs, bw:])
        log_a = tr * chalf_k + chalf_k
        a = jnp.exp(log_a)
        th = jnp.tanh(log_a)
        b = jnp.sqrt(-0.5 * th / (1.0 - th)) * ((ti + 1.0) * ub[rs, :].astype(F32))
        ngrp = LRU_VT // SUBLANES
        a_t = pltpu.einshape("gsl->sgl", a.reshape(ngrp, SUBLANES, bw))
        b_t = pltpu.einshape("gsl->sgl", b.reshape(ngrp, SUBLANES, bw))
        h_loc, p_loc = [None] * SUBLANES, [None] * SUBLANES
        h = p = None
        for s in (range(SUBLANES) if fwd else range(SUBLANES - 1, -1, -1)):
            h = b_t[s] if h is None else a_t[s] * h + b_t[s]
            p = a_t[s] if p is None else a_t[s] * p
            h_loc[s], p_loc[s] = h, p
        cur = c[0:1, :]
        h_in = [None] * ngrp
        for gi in (range(ngrp) if fwd else range(ngrp - 1, -1, -1)):
            h_in[gi] = cur
            cur = p[gi:gi + 1, :] * cur + h[gi:gi + 1, :]
        c = jnp.broadcast_to(cur, (SUBLANES, bw))
        h_in = jnp.concatenate(h_in, axis=0)
        h_t = jnp.stack([h_loc[s] + p_loc[s] * h_in for s in range(SUBLANES)], axis=0)
        hs[si] = pltpu.einshape("sgl->gsl", h_t).reshape(LRU_VT, bw)
    return jnp.concatenate(hs, axis=0), c


def _post_kernel(x_ref, y_ref, z_ref, gate_ref, ylru_ref, mod_ref, nw_ref, wbs_ref, wbl_ref, wout_ref,
                 ln1g_ref, ln1b_ref, w1_ref, b1_ref, w2_ref, b2_ref, ln2g_ref, ln2b_ref, o_ref, *, tb):
    d = D_MODEL
    mod = mod_ref[0]
    gate1 = mod[:, 2 * d:3 * d]
    shift2 = mod[:, 3 * d:4 * d]
    scale2 = mod[:, 4 * d:5 * d]
    gate2 = mod[:, 5 * d:6 * d]

    def slab(rows):
        u = y_ref[0, rows, :].astype(F32) * z_ref[0, rows, :].astype(F32)
        parts = []
        for g in range(SSD_GROUPS):
            ug = u[:, g * SSD_GROUP_W:(g + 1) * SSD_GROUP_W]
            ms = jnp.mean(ug * ug, axis=-1, keepdims=True)
            parts.append(ug * lax.rsqrt(ms + RMS_EPS))
        un = (jnp.concatenate(parts, axis=1) * nw_ref[...]).astype(BF16)
        yield
        br_ssd = jnp.dot(un, wbs_ref[...], preferred_element_type=F32)
        br_lru = jnp.dot(ylru_ref[0, rows, :], wbl_ref[...], preferred_element_type=F32)
        gates = gate_ref[0, rows, :].astype(F32)
        merged = (gates[:, :d] * br_ssd + gates[:, d:] * br_lru).astype(BF16)
        yield
        x_mix = jnp.dot(merged, wout_ref[...], preferred_element_type=F32)
        x1 = _layer_norm(DEEPNORM_ALPHA * x_ref[0, rows, :] + gate1 * x_mix) * ln1g_ref[...] + ln1b_ref[...]
        h2 = (_layer_norm(x1) * (1.0 + scale2) + shift2).astype(BF16)
        yield
        hid = jnp.dot(h2, w1_ref[...], preferred_element_type=F32) + b1_ref[...]
        hid = jnp.maximum(hid, 0.0)
        hid = (hid * hid).astype(BF16)
        yield
        mlp = jnp.dot(hid, w2_ref[...], preferred_element_type=F32) + b2_ref[...]
        o_ref[0, rows, :] = _layer_norm(DEEPNORM_ALPHA * x1 + gate2 * mlp) * ln2g_ref[...] + ln2b_ref[...]
        yield

    gens = [slab(pl.ds(r * POST_SUB, POST_SUB)) for r in range(tb // POST_SUB)]
    nstage = 5
    for tick in range(nstage + len(gens) - 1):
        for r, gen in enumerate(gens):
            if 0 <= tick - r < nstage:
                next(gen)


def _const_spec(shape):
    nd = len(shape)
    return pl.BlockSpec(shape, lambda b, i: (0,) * nd, pipeline_mode=pl.Buffered(1))


def _post(x, y_ssd, proj, y_lru, mod, norm_w, w_br_ssd, w_br_lru, w_out, ln1_g, ln1_b,
          w1, b1, w2, b2, ln2_g, ln2_b, *, tb, z_col, gate_col):
    bsz, t, d = x.shape
    di = SSD_D_INNER
    row = lambda v: v.reshape(1, -1)
    return pl.pallas_call(
        functools.partial(_post_kernel, tb=tb),
        grid=(bsz, t // tb),
        in_specs=[pl.BlockSpec((1, tb, d), lambda b, i: (b, i, 0)),
                  pl.BlockSpec((1, tb, di), lambda b, i: (b, i, 0)),
                  pl.BlockSpec((1, tb, di), lambda b, i: (b, i, z_col)),
                  pl.BlockSpec((1, tb, di), lambda b, i: (b, i, gate_col)),
                  pl.BlockSpec((1, tb, d), lambda b, i: (b, i, 0)),
                  pl.BlockSpec((1, 1, N_MOD * d), lambda b, i: (b, 0, 0)),
                  _const_spec((1, di)),
                  _const_spec((di, d)),
                  _const_spec((d, d)),
                  _const_spec((d, d)),
                  _const_spec((1, d)),
                  _const_spec((1, d)),
                  _const_spec((d, MLP_HIDDEN)),
                  _const_spec((1, MLP_HIDDEN)),
                  _const_spec((MLP_HIDDEN, d)),
                  _const_spec((1, d)),
                  _const_spec((1, d)),
                  _const_spec((1, d))],
        out_specs=pl.BlockSpec((1, tb, d), lambda b, i: (b, i, 0)),
        out_shape=jax.ShapeDtypeStruct((bsz, t, d), F32),
        compiler_params=pltpu.CompilerParams(
            dimension_semantics=("parallel", "parallel"),
            vmem_limit_bytes=VMEM_LIMIT),
        name="post",
    )(x, y_ssd, proj, proj, y_lru, mod, row(norm_w), w_br_ssd.astype(BF16), w_br_lru.astype(BF16),
      w_out.astype(BF16), row(ln1_g), row(ln1_b), w1.astype(BF16), row(b1), w2.astype(BF16), row(b2),
      row(ln2_g), row(ln2_b))


def _pad_rows(v, n):
    return jnp.pad(v, ((0, n - v.shape[0]), (0, 0)))


def kernel(x, c, ctx, c_ctx, w_mod, b_mod, w_in, b_gate, ssd_conv_w, ssd_conv_b, ssd_dt_bias, ssd_a_log,
           ssd_d, ssd_norm_w, lru_conv_w, lru_conv_b, lru_wa, lru_ba, lru_wi, lru_bi, lru_lambda, w_br_ssd,
           w_br_lru, w_out, ln1_g, ln1_b, w_mlp1, b_mlp1, w_mlp2, b_mlp2, ln2_g, ln2_b):
    bsz, t, d = x.shape
    tctx = ctx.shape[1]
    l = 0
    wi = w_in[l]

    nrow = -(-(bsz + 1) // SUBLANES) * SUBLANES
    cvecs = _pad_rows(jnp.concatenate([c, c_ctx[None, :]], axis=0), nrow)
    mods = _modulation(cvecs, w_mod[l], b_mod[l])
    mod_x = mods[:bsz].reshape(bsz, 1, N_MOD * d)
    shift_x, scale_x = mod_x[:, :, :d], mod_x[:, :, d:2 * d]
    shift_c = jnp.broadcast_to(mods[bsz:bsz + 1, :d].reshape(1, 1, d), (bsz, 1, d))
    scale_c = jnp.broadcast_to(mods[bsz:bsz + 1, d:2 * d].reshape(1, 1, d), (bsz, 1, d))

    zc = lambda n: jnp.zeros((CONV_K, n), F32)
    z1 = lambda n: jnp.zeros((n,), F32)
    w_x = jnp.concatenate([wi[:, :SSD_XB], wi[:, O_C:O_Z], wi[:, O_Z:O_LRU_GATE], wi[:, O_MERGE:],
                           wi[:, O_LRU:STATE_COLS], wi[:, O_LRU_GATE:O_MERGE]], axis=1).astype(BF16)
    cw_x = _pad_rows(jnp.concatenate([ssd_conv_w[l], zc(SSD_D_INNER), zc(2 * d), lru_conv_w[l], zc(LRU_WIDTH)],
                                     axis=1), SUBLANES)
    cb_x = jnp.concatenate([ssd_conv_b[l], z1(SSD_D_INNER), b_gate[l], lru_conv_b[l], z1(LRU_WIDTH)])[None, :]
    n_xbc, n_z, n_m, n_l = SSD_XBC // PROJ_TN, SSD_D_INNER // PROJ_TN, 2 * d // PROJ_TN, LRU_WIDTH // PROJ_TN
    segs_x, lo = [], 0
    for n, kind in ((n_xbc, "conv_silu"), (n_z, "silu"), (n_m, "sigmoid_bias"), (n_l, "conv"), (n_l, "gelu")):
        segs_x.append((lo, lo + n, kind))
        lo += n
    z_col = SSD_XBC // SSD_D_INNER
    gate_col = z_col + 1
    lru_col = (SSD_XBC + 2 * SSD_D_INNER) // LRU_WIDTH
    lrug_col = lru_col + 1

    w_c = jnp.concatenate([wi[:, :SSD_XB], wi[:, O_LRU:STATE_COLS]], axis=1).astype(BF16)
    cw_c = _pad_rows(jnp.concatenate([ssd_conv_w[l][:, :SSD_XB], lru_conv_w[l]], axis=1), SUBLANES)
    cb_c = jnp.concatenate([ssd_conv_b[l][:SSD_XB], lru_conv_b[l]])[None, :]
    segs_c = [(0, SSD_XB // PROJ_TN, "conv_silu"), (SSD_XB // PROJ_TN, SSD_XB // PROJ_TN + n_l, "conv")]
    lru_col_c = SSD_XB // LRU_WIDTH

    w_dt = jnp.pad(wi[:, O_DT:O_LRU], ((0, 0), (0, DT_W - SSD_DT)))
    w_dt_hi = w_dt.astype(BF16)
    w_dt_lo = (w_dt - w_dt_hi.astype(F32)).astype(BF16)
    w_dt2 = jnp.concatenate([w_dt_hi, w_dt_lo], axis=1)
    dt_bias = jnp.pad(ssd_dt_bias[l].reshape(1, SSD_DT), ((0, 0), (0, DT_W - SSD_DT)))
    alog_row = jnp.pad(ssd_a_log[l].reshape(1, SSD_DT), ((0, 0), (0, DT_W - SSD_DT)))
    dskip = jnp.repeat(ssd_d[l], SSD_HEADDIM)[None, :]
    hexp = jnp.repeat(jnp.eye(SSD_HEADS, dtype=BF16), SSD_HEADDIM, axis=1)

    w_blk = (0.5 * jnp.concatenate([lru_wa[l, 0], lru_wi[l, 0], lru_wa[l, 1], lru_wi[l, 1]], axis=-1)).astype(BF16)
    blk = lambda v: v.reshape(LRU_BLOCKS, LRU_BLOCK_W)
    gb_blk = 0.5 * jnp.concatenate([blk(lru_ba[l, 0]), blk(lru_bi[l, 0]), blk(lru_ba[l, 1]), blk(lru_bi[l, 1])],
                                   axis=-1)
    lam = lru_lambda[l]

    proj_c, dt_c = _inproj(ctx, shift_c, scale_c, w_c, cw_c, cb_c, w_dt2, dt_bias,
                           segs=segs_c, row_len=tctx, tb=tctx)
    lru_c = (proj_c, lru_col_c, None, w_blk, gb_blk, lam, None)
    s_f, l_f = _ssd_sweep(proj_c, dt_c, alog_row, None, hexp, direction="fwd", mode="state", xbc_width=SSD_XB,
                          lru=lru_c)
    s_b, l_b = _ssd_sweep(proj_c, dt_c, alog_row, None, hexp, direction="bwd", mode="state", xbc_width=SSD_XB,
                          lru=lru_c)

    proj_x, dt_x = _inproj(x, shift_x, scale_x, w_x, cw_x, cb_x, w_dt2, dt_bias,
                           segs=segs_x, row_len=GRID_W, tb=min(t, PROJ_TB))
    y_b, h_b = _ssd_sweep(proj_x, dt_x, alog_row, s_b, hexp, direction="bwd", mode="inter", xbc_width=SSD_XBC,
                          lru=(proj_x, lru_col, lrug_col, w_blk, gb_blk, lam, l_b))
    y_ssd, y_lru = _ssd_sweep(proj_x, dt_x, alog_row, s_f, hexp, direction="fwd", mode="full", xbc_width=SSD_XBC,
                              lru=(proj_x, lru_col, lrug_col, w_blk, gb_blk, lam, l_f),
                              dskip=dskip, yin=y_b, hin=h_b)
    return _post(x, y_ssd, proj_x, y_lru, mod_x, ssd_norm_w[l], w_br_ssd[l], w_br_lru[l], w_out[l],
                 ln1_g[l], ln1_b[l], w_mlp1[l], b_mlp1[l], w_mlp2[l], b_mlp2[l], ln2_g[l], ln2_b[l],
                 tb=min(t, POST_TB), z_col=z_col, gate_col=gate_col)
```

```python
import functools

import jax
import jax.numpy as jnp
from jax import lax
from jax.experimental import pallas as pl
from jax.experimental.pallas import tpu as pltpu

F32 = jnp.float32
BF16 = jnp.bfloat16
HIGHEST = lax.Precision.HIGHEST

D_MODEL = 1024
GRID_W = 64
SSD_D_INNER = 2 * D_MODEL
SSD_HEADDIM = 64
SSD_HEADS = SSD_D_INNER // SSD_HEADDIM
SSD_GROUPS = 8
SSD_HPG = SSD_HEADS // SSD_GROUPS
SSD_STATE = 128
SSD_CHUNK = 128
SSD_GROUP_W = SSD_HPG * SSD_HEADDIM
CONV_K = 4
LRU_WIDTH = D_MODEL
LRU_BLOCKS = 8
LRU_BLOCK_W = LRU_WIDTH // LRU_BLOCKS
LRU_C = 8.0
MLP_HIDDEN = 4 * D_MODEL
N_MOD = 6
DEPTH = 1
DEEPNORM_ALPHA = (2 * DEPTH) ** 0.25
LN_EPS = 1e-6
RMS_EPS = 1e-5
LOG2E = 1.4426950408889634
assert SSD_CHUNK // 2 == SSD_HEADDIM

SSD_BC_W = SSD_GROUPS * SSD_STATE
SSD_XB = SSD_D_INNER + SSD_BC_W
SSD_XBC = SSD_D_INNER + 2 * SSD_BC_W
SSD_DT = 2 * SSD_HEADS
O_DT = SSD_XB
O_LRU = O_DT + SSD_DT
STATE_COLS = O_LRU + LRU_WIDTH
O_C = STATE_COLS
O_Z = O_C + SSD_BC_W
O_LRU_GATE = O_Z + SSD_D_INNER
O_MERGE = O_LRU_GATE + LRU_WIDTH

LANES = 128
SUBLANES = 8
VMEM_LIMIT = 58 * 1024 * 1024

PROJ_TN = 1024
PROJ_SUB = 256
PROJ_TB = 512
CONV_GAP = SUBLANES
DT_W = LANES
SSD_TB = 256
SSD_NB = 2
LRU_TC = SSD_CHUNK
LRU_VT = 64
POST_TB = 512
POST_SUB = 256


def _silu(v):
    return v / (1.0 + jnp.exp(-v))


def _sigmoid(v):
    return 1.0 / (1.0 + jnp.exp(-v))


def _softplus(v):
    return jnp.maximum(v, 0.0) + jnp.log1p(jnp.exp(-jnp.abs(v)))


def _layer_norm(v):
    mu = jnp.mean(v, axis=-1, keepdims=True)
    vc = v - mu
    var = jnp.mean(vc * vc, axis=-1, keepdims=True)
    return vc * lax.rsqrt(var + LN_EPS)


def _split3_bf16(v):
    p1 = v.astype(BF16)
    r1 = v - p1.astype(F32)
    p2 = r1.astype(BF16)
    p3 = (r1 - p2.astype(F32)).astype(BF16)
    return p1, p2, p3


def _tri_rows(tri, v):
    w = v.shape[1]
    out = jnp.dot(tri, jnp.concatenate(_split3_bf16(v), axis=1), preferred_element_type=F32)
    return out[:, :w] + out[:, w:2 * w] + out[:, 2 * w:]


def _tri_cols(v, tri):
    h = v.shape[0]
    out = jnp.dot(jnp.concatenate(_split3_bf16(v), axis=0), tri, preferred_element_type=F32)
    return out[:h, :] + out[h:2 * h, :] + out[2 * h:, :]


def _mod_kernel(c_ref, w_ref, b_ref, o_ref):
    s = _silu(c_ref[...])
    o_ref[...] = jnp.dot(s, w_ref[...], precision=HIGHEST, preferred_element_type=F32) + b_ref[...]


def _modulation(cvecs, w_mod, b_mod):
    rows, ncols = cvecs.shape[0], w_mod.shape[1]
    tn = 512
    return pl.pallas_call(
        _mod_kernel,
        grid=(ncols // tn,),
        in_specs=[pl.BlockSpec((rows, D_MODEL), lambda j: (0, 0)),
                  pl.BlockSpec((D_MODEL, tn), lambda j: (0, j)),
                  pl.BlockSpec((1, tn), lambda j: (0, j))],
        out_specs=pl.BlockSpec((rows, tn), lambda j: (0, j)),
        out_shape=jax.ShapeDtypeStruct((rows, ncols), F32),
        name="modulation",
    )(cvecs, w_mod, b_mod.reshape(1, ncols))


def _inproj_kernel(x_ref, shift_ref, scale_ref, w_ref, cw_ref, cb_ref, wdt_ref, dtb_ref,
                   p_ref, dt_ref, h_scr, c_scr, *, segs, row_len, tb):
    sub = PROJ_SUB
    npiece = sub // row_len
    pitch = row_len + CONV_GAP
    zgap = jnp.zeros((CONV_GAP, PROJ_TN), F32)
    for r in range(npiece + 1):
        c_scr[r * pitch:r * pitch + CONV_GAP, :] = zgap

    def sub_block(s, carry):
        rows = pl.ds(pl.multiple_of(s * sub, sub), sub)
        h = _layer_norm(x_ref[0, rows, :]) * (1.0 + scale_ref[0]) + shift_ref[0]
        hb = h.astype(BF16)
        h_scr[...] = hb
        dt2 = jnp.dot(hb, wdt_ref[...], preferred_element_type=F32)
        dt_ref[0, rows, :] = _softplus(dt2[:, :DT_W] + dt2[:, DT_W:] + dtb_ref[...])
        for lo, hi, kind in segs:
            for j in range(lo, hi):
                cols = slice(j * PROJ_TN, (j + 1) * PROJ_TN)
                acc = jnp.dot(h_scr[...], w_ref[:, cols], preferred_element_type=F32)
                cb = cb_ref[:, cols]
                if kind in ("conv_silu", "conv"):
                    cw = cw_ref[:, cols]
                    for r in range(npiece):
                        base = CONV_GAP + r * pitch
                        c_scr[base:base + row_len, :] = acc[r * row_len:(r + 1) * row_len, :]
                    pieces = []
                    for r in range(npiece):
                        base = CONV_GAP + r * pitch
                        o = cb + acc[r * row_len:(r + 1) * row_len, :] * cw[2:3]
                        o = o + c_scr[base - 2:base - 2 + row_len, :] * cw[0:1]
                        o = o + c_scr[base - 1:base - 1 + row_len, :] * cw[1:2]
                        o = o + c_scr[base + 1:base + 1 + row_len, :] * cw[3:4]
                        pieces.append(o)
                    out = pieces[0] if npiece == 1 else jnp.concatenate(pieces, axis=0)
                    if kind == "conv_silu":
                        out = _silu(out)
                elif kind == "silu":
                    out = _silu(acc)
                elif kind == "gelu":
                    out = 0.5 * acc * (1.0 + jnp.tanh(0.7978845608028654 * (acc + 0.044715 * acc * acc * acc)))
                elif kind == "sigmoid_bias":
                    out = _sigmoid(acc + cb)
                else:
                    raise ValueError(kind)
                p_ref[0, rows, cols] = out.astype(BF16)
        return carry

    lax.fori_loop(0, tb // sub, sub_block, 0)


def _inproj(xs, shift, scale, w_all, cw_all, cb_all, w_dt2, dt_bias, *, segs, row_len, tb):
    bsz, t, _ = xs.shape
    ncols = w_all.shape[1]
    kern = functools.partial(_inproj_kernel, segs=segs, row_len=row_len, tb=tb)
    const = lambda shape: pl.BlockSpec(shape, lambda b, i: (0, 0), pipeline_mode=pl.Buffered(1))
    conv_rows = CONV_GAP + (PROJ_SUB // row_len) * (row_len + CONV_GAP)
    return pl.pallas_call(
        kern,
        grid=(bsz, t // tb),
        in_specs=[pl.BlockSpec((1, tb, D_MODEL), lambda b, i: (b, i, 0)),
                  pl.BlockSpec((1, 1, D_MODEL), lambda b, i: (b, 0, 0)),
                  pl.BlockSpec((1, 1, D_MODEL), lambda b, i: (b, 0, 0)),
                  const((D_MODEL, ncols)),
                  const((SUBLANES, ncols)),
                  const((1, ncols)),
                  const((D_MODEL, 2 * DT_W)),
                  const((1, DT_W))],
        out_specs=[pl.BlockSpec((1, tb, ncols), lambda b, i: (b, i, 0)),
                   pl.BlockSpec((1, tb, DT_W), lambda b, i: (b, i, 0))],
        out_shape=[jax.ShapeDtypeStruct((bsz, t, ncols), BF16),
                   jax.ShapeDtypeStruct((bsz, t, DT_W), F32)],
        scratch_shapes=[pltpu.VMEM((PROJ_SUB, D_MODEL), BF16),
                        pltpu.VMEM((conv_rows, PROJ_TN), F32)],
        compiler_params=pltpu.CompilerParams(
            dimension_semantics=("parallel", "parallel"),
            vmem_limit_bytes=VMEM_LIMIT),
        name="inproj",
    )(xs, shift, scale, w_all, cw_all, cb_all, w_dt2, dt_bias)


def _ssd_kernel(*refs, direction, mode, nsteps, nsub, nbatch):
    if mode == "full":
        (xbc_ref, dt_ref, alog_ref, h0_ref, hexp_ref, u_ref, lw_ref, lgb_ref, lam_ref, lh0_ref,
         dskip_ref, yin_ref, hin_ref, g_ref,
         y_ref, hout_ref, s_scr, e_scr, lc_scr, y_scr, cb_scr) = refs
    elif mode == "inter":
        (xbc_ref, dt_ref, alog_ref, h0_ref, hexp_ref, u_ref, lw_ref, lgb_ref, lam_ref, lh0_ref,
         y_ref, hout_ref, s_scr, e_scr, lc_scr) = refs
    else:
        (xbc_ref, dt_ref, alog_ref, hexp_ref, u_ref, lw_ref, lgb_ref, lam_ref,
         sfin_ref, lfin_ref, s_scr, e_scr, lc_scr) = refs
    step = pl.program_id(1)
    L = SSD_CHUNK
    fwd = direction == "fwd"
    groups = [slice(g * SSD_GROUP_W, (g + 1) * SSD_GROUP_W) for g in range(SSD_GROUPS)]
    bcols = [slice(SSD_D_INNER + g * SSD_STATE, SSD_D_INNER + (g + 1) * SSD_STATE) for g in range(SSD_GROUPS)]
    ccols = [slice(SSD_XB + g * SSD_STATE, SSD_XB + (g + 1) * SSD_STATE) for g in range(SSD_GROUPS)]

    @pl.when(step == 0)
    def _():
        if mode == "state":
            s_scr[...] = jnp.zeros(s_scr.shape, F32)
            lc_scr[...] = jnp.zeros(lc_scr.shape, F32)
        else:
            s_scr[...] = h0_ref[...]
            for bb in range(nbatch):
                lc_scr[bb] = jnp.broadcast_to(lh0_ref[bb], (SUBLANES, LRU_WIDTH))

    ldir = 0 if fwd else 1
    chalf = _lru_half_log_a(lam_ref[ldir:ldir + 1, :])
    ldcols = slice(ldir * 2 * LRU_BLOCK_W, (ldir + 1) * 2 * LRU_BLOCK_W)

    def lru_tiles(bb, rows, ks):
        for k in ks:
            cols = slice(k * LRU_BLOCK_W, (k + 1) * LRU_BLOCK_W)
            hv, c = _lru_block(u_ref[bb, rows, cols], lw_ref[k, :, ldcols], lgb_ref[k:k + 1, ldcols],
                               chalf[:, cols], lc_scr[bb, :, cols], fwd)
            lc_scr[bb, :, cols] = c
            if mode == "state":
                continue
            if mode == "full":
                hv = (hv + hin_ref[bb, rows, cols].astype(F32)) * g_ref[bb, rows, cols].astype(F32)
            hout_ref[bb, rows, cols] = hv.astype(hout_ref.dtype)

    a_neg = -jnp.exp(alog_ref[...]) * LOG2E
    ri = lax.broadcasted_iota(jnp.int32, (L, L), 0)
    ci = lax.broadcasted_iota(jnp.int32, (L, L), 1)
    tril = jnp.where(ri >= ci, 1.0, 0.0).astype(BF16)
    triu = jnp.where(ri <= ci, 1.0, 0.0).astype(BF16)

    def expand(vals, col0):
        pieces = _split3_bf16(vals[:, col0:col0 + SSD_HEADS])
        return jnp.dot(jnp.concatenate(pieces, axis=1), hexp_ref[...], preferred_element_type=F32)

    def chunk(i, bb):
        cidx = i if fwd else nsub - 1 - i
        rows = pl.ds(cidx * L, L)
        es = e_scr.at[bb, i]
        ss = s_scr.at[bb]
        if mode == "full":
            ys = y_scr.at[bb, i]
            cs = cb_scr.at[bb, i]
        dt = dt_ref[bb, rows, :]
        dta = dt * a_neg
        cum = _tri_rows(tril if fwd else triu, dta)
        total = cum[L - 1:L, :] if fwd else cum[0:1, :]
        w_state = dt * jnp.exp2(total - cum)
        dec_tot = jnp.broadcast_to(jnp.exp2(total), (SUBLANES, DT_W))
        if mode == "state":
            stack = jnp.concatenate([w_state, dec_tot], axis=0)
        elif mode == "inter":
            stack = jnp.concatenate([w_state, dec_tot, jnp.exp2(cum)], axis=0)
        else:
            stack = jnp.concatenate([w_state, dec_tot, cum], axis=0)
        lru_tiles(bb, rows, (0, 1))
        yield
        es[...] = expand(stack, 0 if fwd else SSD_HEADS)
        r_dec = L
        r_ein = L + SUBLANES
        lru_tiles(bb, rows, (2, 3))
        if mode == "full":
            P = SSD_HEADDIM
            half = L // 2
            cum_b = _tri_rows(triu, dta)
            cs[...] = expand(cum_b, SSD_HEADS)
            dta_t = dta.T
            ldt_t = jnp.log2(dt.T)
            row_f = ldt_t - _tri_cols(dta_t, triu)
            row_b = ldt_t - _tri_cols(dta_t, tril)
            ti = lax.broadcasted_iota(jnp.int32, (half, 2 * P), 0)
            tj = lax.broadcasted_iota(jnp.int32, (half, 2 * P), 1) % P
            lo_tri = ti >= tj
            up_tri = ti <= tj
            lane_a = lax.broadcasted_iota(jnp.int32, (half, 2 * P), 1) < P

            def intra_group(g):
                bg = xbc_ref[bb, rows, bcols[g]]
                cg = xbc_ref[bb, rows, ccols[g]]
                b4 = jnp.concatenate([bg[:half, :], bg[:half, :], bg[half:, :], bg[half:, :]], axis=0)
                cb4 = lax.dot_general(cg, b4, (((1,), (1,)), ((), ())), preferred_element_type=F32)
                for pp in range(SSD_HPG // 2):
                    ha = g * SSD_HPG + 2 * pp
                    hb = SSD_HEADS + ha
                    px = slice(ha * P, (ha + 2) * P)
                    col_f = es[r_ein:r_ein + L, px]
                    col_b = cs[:, px]
                    xp = xbc_ref[bb, rows, px]
                    ms, rhss = [], []
                    for jh in range(2):
                        js = slice(jh * half, (jh + 1) * half)
                        rf = jnp.concatenate([row_f[ha:ha + 1, js], row_f[ha + 1:ha + 2, js]], axis=1)
                        rb = jnp.concatenate([row_b[hb:hb + 1, js], row_b[hb + 1:hb + 2, js]], axis=1)
                        dg = slice(jh * half, (jh + 1) * half)
                        zb = jnp.zeros((half, 2 * P), BF16)
                        m_diag = (jnp.where(lo_tri, jnp.exp2(col_f[dg, :] + rf).astype(BF16), zb)
                                  + jnp.where(up_tri, jnp.exp2(col_b[dg, :] + rb).astype(BF16), zb))
                        if jh == 0:
                            m = jnp.concatenate([m_diag, jnp.exp2(col_f[half:, :] + rf).astype(BF16)], axis=0)
                        else:
                            m = jnp.concatenate([jnp.exp2(col_b[:half, :] + rb).astype(BF16), m_diag], axis=0)
                        ms.append(m * cb4[:, jh * 2 * P:(jh + 1) * 2 * P].astype(BF16))
                        xj = xp[js, :]
                        zero = jnp.zeros_like(xj)
                        rhss += [jnp.where(lane_a, xj, zero), jnp.where(lane_a, zero, xj)]
                    y2 = jnp.dot(jnp.concatenate(ms, axis=1), jnp.concatenate(rhss, axis=0),
                                 preferred_element_type=F32)
                    y_ref[bb, rows, px] = (ys[:, px] + y2).astype(y_ref.dtype)

        yield

        if mode != "state":
            for g, gx in enumerate(groups):
                cg = xbc_ref[bb, rows, ccols[g]]
                e_in = es[r_ein:r_ein + L, gx]
                if mode == "full":
                    e_in = jnp.exp2(e_in)
                y_g = e_in * jnp.dot(cg, ss[:, gx].astype(BF16), preferred_element_type=F32)
                if mode == "inter":
                    y_ref[bb, rows, gx] = y_g.astype(y_ref.dtype)
                else:
                    xg = xbc_ref[bb, rows, gx].astype(F32)
                    ys[:, gx] = y_g + yin_ref[bb, rows, gx].astype(F32) + dskip_ref[:, gx] * xg

        lru_tiles(bb, rows, (4, 5))
        yield
        if mode == "full":
            for g in range(SSD_GROUPS // 2):
                intra_group(g)
                yield
        for g, gx in enumerate(groups):
            bg = xbc_ref[bb, rows, bcols[g]]
            xw = xbc_ref[bb, rows, gx] * es[0:L, gx].astype(BF16)
            upd = lax.dot_general(bg, xw, (((0,), (0,)), ((), ())), preferred_element_type=F32)
            ss[:, gx] = es[r_dec:r_dec + 1, gx] * ss[:, gx] + upd
        lru_tiles(bb, rows, (6, 7))
        yield

        if mode == "full":
            for g in range(SSD_GROUPS // 2, SSD_GROUPS):
                intra_group(g)
                yield

    gens = [[chunk(i, bb) for bb in range(nbatch)] for i in range(nsub)]
    for _ in range(2):
        for i in range(nsub):
            for gen in gens[i]:
                next(gen)
    for i in range(nsub):
        for _ in zip(*gens[i]):
            pass

    if mode == "state":
        @pl.when(step == nsteps - 1)
        def _():
            sfin_ref[...] = s_scr[...]
            for bb in range(nbatch):
                lfin_ref[bb] = lc_scr[bb, 0:1, :]


def _ssd_sweep(xbc, dt, alog_row, h0, hexp, *, direction, mode, xbc_width, lru=None, dskip=None, yin=None,
               hin=None):
    bsz, t, _ = xbc.shape
    tb = min(t, SSD_TB)
    nb = SSD_NB if bsz % SSD_NB == 0 else 1
    nsteps = t // tb
    hp = SSD_D_INNER
    w = LRU_WIDTH
    row = (lambda c: c) if direction == "fwd" else (lambda c: nsteps - 1 - c)
    cmap = lambda b, c: (b, row(c), 0)
    const = lambda shape: pl.BlockSpec(shape, lambda b, c: (0,) * len(shape))
    in_specs = [pl.BlockSpec((nb, tb, xbc_width), cmap),
                pl.BlockSpec((nb, tb, DT_W), cmap),
                const((1, DT_W))]
    args = [xbc, dt, alog_row]
    if mode != "state":
        in_specs.append(pl.BlockSpec((nb, SSD_STATE, hp), lambda b, c: (b, 0, 0)))
        args.append(h0)
    in_specs.append(const((3 * SSD_HEADS, hp)))
    args.append(jnp.concatenate([hexp, hexp, hexp], axis=0))
    stack_rows = (SSD_CHUNK + SUBLANES) if mode == "state" else (2 * SSD_CHUNK + SUBLANES)
    nsub = tb // SSD_CHUNK
    scratch = [pltpu.VMEM((nb, SSD_STATE, hp), F32), pltpu.VMEM((nb, nsub, stack_rows, hp), F32)]
    proj, u_col, g_col, w_blk, gb_blk, lam, lh0 = lru
    in_specs += [pl.BlockSpec((nb, tb, w), lambda b, c: (b, row(c), u_col)),
                 const((LRU_BLOCKS, LRU_BLOCK_W, 4 * LRU_BLOCK_W)),
                 const((LRU_BLOCKS, 4 * LRU_BLOCK_W)),
                 const((2, w))]
    args += [proj, w_blk, gb_blk, lam]
    if mode != "state":
        in_specs.append(pl.BlockSpec((nb, 1, w), lambda b, c: (b, 0, 0)))
        args.append(lh0)
    scratch.append(pltpu.VMEM((nb, SUBLANES, w), F32))
    if mode == "full":
        scratch += [pltpu.VMEM((nb, nsub, SSD_CHUNK, hp), F32), pltpu.VMEM((nb, nsub, SSD_CHUNK, hp), F32)]
        in_specs += [const((1, hp)), pl.BlockSpec((nb, tb, hp), cmap), pl.BlockSpec((nb, tb, w), cmap),
                     pl.BlockSpec((nb, tb, w), lambda b, c: (b, row(c), g_col))]
        args += [dskip, yin, hin, proj]
    if mode == "state":
        out_specs = [pl.BlockSpec((nb, SSD_STATE, hp), lambda b, c: (b, 0, 0)),
                     pl.BlockSpec((nb, 1, w), lambda b, c: (b, 0, 0))]
        out_shape = [jax.ShapeDtypeStruct((bsz, SSD_STATE, hp), F32), jax.ShapeDtypeStruct((bsz, 1, w), F32)]
    else:
        out_specs = [pl.BlockSpec((nb, tb, hp), cmap), pl.BlockSpec((nb, tb, w), cmap)]
        out_shape = [jax.ShapeDtypeStruct((bsz, t, hp), BF16), jax.ShapeDtypeStruct((bsz, t, w), BF16)]
    kern = functools.partial(_ssd_kernel, direction=direction, mode=mode, nsteps=nsteps, nsub=nsub, nbatch=nb)
    return pl.pallas_call(
        kern,
        grid=(bsz // nb, nsteps),
        in_specs=in_specs,
        out_specs=out_specs,
        out_shape=out_shape,
        scratch_shapes=scratch,
        compiler_params=pltpu.CompilerParams(
            dimension_semantics=("parallel", "arbitrary"),
            vmem_limit_bytes=VMEM_LIMIT),
        name=f"ssd_{direction}_{mode}",
    )(*args)


def _lru_half_log_a(lam_row):
    return (-0.5 * LRU_C) * _softplus(-lam_row)


def _lru_block(ub, wk, gbk, chalf_k, c, fwd):
    bw = LRU_BLOCK_W
    pre = jnp.dot(ub, wk, preferred_element_type=F32) + gbk
    nsl = LRU_TC // LRU_VT
    hs = [None] * nsl
    for si in (range(nsl) if fwd else range(nsl - 1, -1, -1)):
        rs = slice(si * LRU_VT, (si + 1) * LRU_VT)
        tr = jnp.tanh(pre[rs, :bw])
        ti = jnp.tanh(pre[rs, bw:])
        log_a = tr * chalf_k + chalf_k
        a = jnp.exp(log_a)
        th = jnp.tanh(log_a)
        b = jnp.sqrt(-0.5 * th / (1.0 - th)) * ((ti + 1.0) * ub[rs, :].astype(F32))
        ngrp = LRU_VT // SUBLANES
        a_t = pltpu.einshape("gsl->sgl", a.reshape(ngrp, SUBLANES, bw))
        b_t = pltpu.einshape("gsl->sgl", b.reshape(ngrp, SUBLANES, bw))
        h_loc, p_loc = [None] * SUBLANES, [None] * SUBLANES
        h = p = None
        for s in (range(SUBLANES) if fwd else range(SUBLANES - 1, -1, -1)):
            h = b_t[s] if h is None else a_t[s] * h + b_t[s]
            p = a_t[s] if p is None else a_t[s] * p
            h_loc[s], p_loc[s] = h, p
        cur = c[0:1, :]
        h_in = [None] * ngrp
        for gi in (range(ngrp) if fwd else range(ngrp - 1, -1, -1)):
            h_in[gi] = cur
            cur = p[gi:gi + 1, :] * cur + h[gi:gi + 1, :]
        c = jnp.broadcast_to(cur, (SUBLANES, bw))
        h_in = jnp.concatenate(h_in, axis=0)
        h_t = jnp.stack([h_loc[s] + p_loc[s] * h_in for s in range(SUBLANES)], axis=0)
        hs[si] = pltpu.einshape("sgl->gsl", h_t).reshape(LRU_VT, bw)
    return jnp.concatenate(hs, axis=0), c


def _post_kernel(x_ref, y_ref, z_ref, gate_ref, ylru_ref, mod_ref, nw_ref, wbs_ref, wbl_ref, wout_ref,
                 ln1g_ref, ln1b_ref, w1_ref, b1_ref, w2_ref, b2_ref, ln2g_ref, ln2b_ref, o_ref, *, tb):
    d = D_MODEL
    mod = mod_ref[0]
    gate1 = mod[:, 2 * d:3 * d]
    shift2 = mod[:, 3 * d:4 * d]
    scale2 = mod[:, 4 * d:5 * d]
    gate2 = mod[:, 5 * d:6 * d]

    def slab(rows):
        u = y_ref[0, rows, :].astype(F32) * z_ref[0, rows, :].astype(F32)
        parts = []
        for g in range(SSD_GROUPS):
            ug = u[:, g * SSD_GROUP_W:(g + 1) * SSD_GROUP_W]
            ms = jnp.mean(ug * ug, axis=-1, keepdims=True)
            parts.append(ug * lax.rsqrt(ms + RMS_EPS))
        un = (jnp.concatenate(parts, axis=1) * nw_ref[...]).astype(BF16)
        yield
        br_ssd = jnp.dot(un, wbs_ref[...], preferred_element_type=F32)
        br_lru = jnp.dot(ylru_ref[0, rows, :], wbl_ref[...], preferred_element_type=F32)
        gates = gate_ref[0, rows, :].astype(F32)
        merged = (gates[:, :d] * br_ssd + gates[:, d:] * br_lru).astype(BF16)
        yield
        x_mix = jnp.dot(merged, wout_ref[...], preferred_element_type=F32)
        x1 = _layer_norm(DEEPNORM_ALPHA * x_ref[0, rows, :] + gate1 * x_mix) * ln1g_ref[...] + ln1b_ref[...]
        h2 = (_layer_norm(x1) * (1.0 + scale2) + shift2).astype(BF16)
        yield
        hid = jnp.dot(h2, w1_ref[...], preferred_element_type=F32) + b1_ref[...]
        hid = jnp.maximum(hid, 0.0)
        hid = (hid * hid).astype(BF16)
        yield
        mlp = jnp.dot(hid, w2_ref[...], preferred_element_type=F32) + b2_ref[...]
        o_ref[0, rows, :] = _layer_norm(DEEPNORM_ALPHA * x1 + gate2 * mlp) * ln2g_ref[...] + ln2b_ref[...]
        yield

    gens = [slab(pl.ds(r * POST_SUB, POST_SUB)) for r in range(tb // POST_SUB)]
    nstage = 5
    for tick in range(nstage + len(gens) - 1):
        for r, gen in enumerate(gens):
            if 0 <= tick - r < nstage:
                next(gen)


def _const_spec(shape):
    nd = len(shape)
    return pl.BlockSpec(shape, lambda b, i: (0,) * nd, pipeline_mode=pl.Buffered(1))


def _post(x, y_ssd, proj, y_lru, mod, norm_w, w_br_ssd, w_br_lru, w_out, ln1_g, ln1_b,
          w1, b1, w2, b2, ln2_g, ln2_b, *, tb, z_col, gate_col):
    bsz, t, d = x.shape
    di = SSD_D_INNER
    row = lambda v: v.reshape(1, -1)
    return pl.pallas_call(
        functools.partial(_post_kernel, tb=tb),
        grid=(bsz, t // tb),
        in_specs=[pl.BlockSpec((1, tb, d), lambda b, i: (b, i, 0)),
                  pl.BlockSpec((1, tb, di), lambda b, i: (b, i, 0)),
                  pl.BlockSpec((1, tb, di), lambda b, i: (b, i, z_col)),
                  pl.BlockSpec((1, tb, di), lambda b, i: (b, i, gate_col)),
                  pl.BlockSpec((1, tb, d), lambda b, i: (b, i, 0)),
                  pl.BlockSpec((1, 1, N_MOD * d), lambda b, i: (b, 0, 0)),
                  _const_spec((1, di)),
                  _const_spec((di, d)),
                  _const_spec((d, d)),
                  _const_spec((d, d)),
                  _const_spec((1, d)),
                  _const_spec((1, d)),
                  _const_spec((d, MLP_HIDDEN)),
                  _const_spec((1, MLP_HIDDEN)),
                  _const_spec((MLP_HIDDEN, d)),
                  _const_spec((1, d)),
                  _const_spec((1, d)),
                  _const_spec((1, d))],
        out_specs=pl.BlockSpec((1, tb, d), lambda b, i: (b, i, 0)),
        out_shape=jax.ShapeDtypeStruct((bsz, t, d), F32),
        compiler_params=pltpu.CompilerParams(
            dimension_semantics=("parallel", "parallel"),
            vmem_limit_bytes=VMEM_LIMIT),
        name="post",
    )(x, y_ssd, proj, proj, y_lru, mod, row(norm_w), w_br_ssd.astype(BF16), w_br_lru.astype(BF16),
      w_out.astype(BF16), row(ln1_g), row(ln1_b), w1.astype(BF16), row(b1), w2.astype(BF16), row(b2),
      row(ln2_g), row(ln2_b))


def _pad_rows(v, n):
    return jnp.pad(v, ((0, n - v.shape[0]), (0, 0)))


def kernel(x, c, ctx, c_ctx, w_mod, b_mod, w_in, b_gate, ssd_conv_w, ssd_conv_b, ssd_dt_bias, ssd_a_log,
           ssd_d, ssd_norm_w, lru_conv_w, lru_conv_b, lru_wa, lru_ba, lru_wi, lru_bi, lru_lambda, w_br_ssd,
           w_br_lru, w_out, ln1_g, ln1_b, w_mlp1, b_mlp1, w_mlp2, b_mlp2, ln2_g, ln2_b):
    bsz, t, d = x.shape
    tctx = ctx.shape[1]
    l = 0
    wi = w_in[l]

    nrow = -(-(bsz + 1) // SUBLANES) * SUBLANES
    cvecs = _pad_rows(jnp.concatenate([c, c_ctx[None, :]], axis=0), nrow)
    mods = _modulation(cvecs, w_mod[l], b_mod[l])
    mod_x = mods[:bsz].reshape(bsz, 1, N_MOD * d)
    shift_x, scale_x = mod_x[:, :, :d], mod_x[:, :, d:2 * d]
    shift_c = jnp.broadcast_to(mods[bsz:bsz + 1, :d].reshape(1, 1, d), (bsz, 1, d))
    scale_c = jnp.broadcast_to(mods[bsz:bsz + 1, d:2 * d].reshape(1, 1, d), (bsz, 1, d))

    zc = lambda n: jnp.zeros((CONV_K, n), F32)
    z1 = lambda n: jnp.zeros((n,), F32)
    w_x = jnp.concatenate([wi[:, :SSD_XB], wi[:, O_C:O_Z], wi[:, O_Z:O_LRU_GATE], wi[:, O_MERGE:],
                           wi[:, O_LRU:STATE_COLS], wi[:, O_LRU_GATE:O_MERGE]], axis=1).astype(BF16)
    cw_x = _pad_rows(jnp.concatenate([ssd_conv_w[l], zc(SSD_D_INNER), zc(2 * d), lru_conv_w[l], zc(LRU_WIDTH)],
                                     axis=1), SUBLANES)
    cb_x = jnp.concatenate([ssd_conv_b[l], z1(SSD_D_INNER), b_gate[l], lru_conv_b[l], z1(LRU_WIDTH)])[None, :]
    n_xbc, n_z, n_m, n_l = SSD_XBC // PROJ_TN, SSD_D_INNER // PROJ_TN, 2 * d // PROJ_TN, LRU_WIDTH // PROJ_TN
    segs_x, lo = [], 0
    for n, kind in ((n_xbc, "conv_silu"), (n_z, "silu"), (n_m, "sigmoid_bias"), (n_l, "conv"), (n_l, "gelu")):
        segs_x.append((lo, lo + n, kind))
        lo += n
    z_col = SSD_XBC // SSD_D_INNER
    gate_col = z_col + 1
    lru_col = (SSD_XBC + 2 * SSD_D_INNER) // LRU_WIDTH
    lrug_col = lru_col + 1

    w_c = jnp.concatenate([wi[:, :SSD_XB], wi[:, O_LRU:STATE_COLS]], axis=1).astype(BF16)
    cw_c = _pad_rows(jnp.concatenate([ssd_conv_w[l][:, :SSD_XB], lru_conv_w[l]], axis=1), SUBLANES)
    cb_c = jnp.concatenate([ssd_conv_b[l][:SSD_XB], lru_conv_b[l]])[None, :]
    segs_c = [(0, SSD_XB // PROJ_TN, "conv_silu"), (SSD_XB // PROJ_TN, SSD_XB // PROJ_TN + n_l, "conv")]
    lru_col_c = SSD_XB // LRU_WIDTH

    w_dt = jnp.pad(wi[:, O_DT:O_LRU], ((0, 0), (0, DT_W - SSD_DT)))
    w_dt_hi = w_dt.astype(BF16)
    w_dt_lo = (w_dt - w_dt_hi.astype(F32)).astype(BF16)
    w_dt2 = jnp.concatenate([w_dt_hi, w_dt_lo], axis=1)
    dt_bias = jnp.pad(ssd_dt_bias[l].reshape(1, SSD_DT), ((0, 0), (0, DT_W - SSD_DT)))
    alog_row = jnp.pad(ssd_a_log[l].reshape(1, SSD_DT), ((0, 0), (0, DT_W - SSD_DT)))
    dskip = jnp.repeat(ssd_d[l], SSD_HEADDIM)[None, :]
    hexp = jnp.repeat(jnp.eye(SSD_HEADS, dtype=BF16), SSD_HEADDIM, axis=1)

    w_blk = (0.5 * jnp.concatenate([lru_wa[l, 0], lru_wi[l, 0], lru_wa[l, 1], lru_wi[l, 1]], axis=-1)).astype(BF16)
    blk = lambda v: v.reshape(LRU_BLOCKS, LRU_BLOCK_W)
    gb_blk = 0.5 * jnp.concatenate([blk(lru_ba[l, 0]), blk(lru_bi[l, 0]), blk(lru_ba[l, 1]), blk(lru_bi[l, 1])],
                                   axis=-1)
    lam = lru_lambda[l]

    proj_c, dt_c = _inproj(ctx, shift_c, scale_c, w_c, cw_c, cb_c, w_dt2, dt_bias,
                           segs=segs_c, row_len=tctx, tb=tctx)
    lru_c = (proj_c, lru_col_c, None, w_blk, gb_blk, lam, None)
    s_f, l_f = _ssd_sweep(proj_c, dt_c, alog_row, None, hexp, direction="fwd", mode="state", xbc_width=SSD_XB,
                          lru=lru_c)
    s_b, l_b = _ssd_sweep(proj_c, dt_c, alog_row, None, hexp, direction="bwd", mode="state", xbc_width=SSD_XB,
                          lru=lru_c)

    proj_x, dt_x = _inproj(x, shift_x, scale_x, w_x, cw_x, cb_x, w_dt2, dt_bias,
                           segs=segs_x, row_len=GRID_W, tb=min(t, PROJ_TB))
    y_b, h_b = _ssd_sweep(proj_x, dt_x, alog_row, s_b, hexp, direction="bwd", mode="inter", xbc_width=SSD_XBC,
                          lru=(proj_x, lru_col, lrug_col, w_blk, gb_blk, lam, l_b))
    y_ssd, y_lru = _ssd_sweep(proj_x, dt_x, alog_row, s_f, hexp, direction="fwd", mode="full", xbc_width=SSD_XBC,
                              lru=(proj_x, lru_col, lrug_col, w_blk, gb_blk, lam, l_f),
                              dskip=dskip, yin=y_b, hin=h_b)
    return _post(x, y_ssd, proj_x, y_lru, mod_x, ssd_norm_w[l], w_br_ssd[l], w_br_lru[l], w_out[l],
                 ln1_g[l], ln1_b[l], w_mlp1[l], b_mlp1[l], w_mlp2[l], b_mlp2[l], ln2_g[l], ln2_b[l],
                 tb=min(t, POST_TB), z_col=z_col, gate_col=gate_col)
```

```python
import functools

import jax
import jax.numpy as jnp
from jax import lax
from jax.experimental import pallas as pl
from jax.experimental.pallas import tpu as pltpu

F32 = jnp.float32
BF16 = jnp.bfloat16
HIGHEST = lax.Precision.HIGHEST

D_MODEL = 1024
GRID_W = 64
SSD_D_INNER = 2 * D_MODEL
SSD_HEADDIM = 64
SSD_HEADS = SSD_D_INNER // SSD_HEADDIM
SSD_GROUPS = 8
SSD_HPG = SSD_HEADS // SSD_GROUPS
SSD_STATE = 128
SSD_CHUNK = 128
SSD_GROUP_W = SSD_HPG * SSD_HEADDIM
CONV_K = 4
LRU_WIDTH = D_MODEL
LRU_BLOCKS = 8
LRU_BLOCK_W = LRU_WIDTH // LRU_BLOCKS
LRU_C = 8.0
MLP_HIDDEN = 4 * D_MODEL
N_MOD = 6
DEPTH = 1
DEEPNORM_ALPHA = (2 * DEPTH) ** 0.25
LN_EPS = 1e-6
RMS_EPS = 1e-5
LOG2E = 1.4426950408889634
assert SSD_CHUNK // 2 == SSD_HEADDIM

SSD_BC_W = SSD_GROUPS * SSD_STATE
SSD_XB = SSD_D_INNER + SSD_BC_W
SSD_XBC = SSD_D_INNER + 2 * SSD_BC_W
SSD_DT = 2 * SSD_HEADS
O_DT = SSD_XB
O_LRU = O_DT + SSD_DT
STATE_COLS = O_LRU + LRU_WIDTH
O_C = STATE_COLS
O_Z = O_C + SSD_BC_W
O_LRU_GATE = O_Z + SSD_D_INNER
O_MERGE = O_LRU_GATE + LRU_WIDTH

LANES = 128
SUBLANES = 8
VMEM_LIMIT = 58 * 1024 * 1024

PROJ_TN = 1024
PROJ_SUB = 256
PROJ_TB = 512
CONV_GAP = SUBLANES
DT_W = LANES
SSD_TB = 256
SSD_NB = 2
LRU_TC = 128
LRU_VT = 64
POST_TB = 512
POST_SUB = 256


def _silu(v):
    return v / (1.0 + jnp.exp(-v))


def _sigmoid(v):
    return 1.0 / (1.0 + jnp.exp(-v))


def _softplus(v):
    return jnp.maximum(v, 0.0) + jnp.log1p(jnp.exp(-jnp.abs(v)))


def _layer_norm(v):
    mu = jnp.mean(v, axis=-1, keepdims=True)
    vc = v - mu
    var = jnp.mean(vc * vc, axis=-1, keepdims=True)
    return vc * lax.rsqrt(var + LN_EPS)


def _split3_bf16(v):
    p1 = v.astype(BF16)
    r1 = v - p1.astype(F32)
    p2 = r1.astype(BF16)
    p3 = (r1 - p2.astype(F32)).astype(BF16)
    return p1, p2, p3


def _tri_rows(tri, v):
    w = v.shape[1]
    out = jnp.dot(tri, jnp.concatenate(_split3_bf16(v), axis=1), preferred_element_type=F32)
    return out[:, :w] + out[:, w:2 * w] + out[:, 2 * w:]


def _tri_cols(v, tri):
    h = v.shape[0]
    out = jnp.dot(jnp.concatenate(_split3_bf16(v), axis=0), tri, preferred_element_type=F32)
    return out[:h, :] + out[h:2 * h, :] + out[2 * h:, :]


def _mod_kernel(c_ref, w_ref, b_ref, o_ref):
    s = _silu(c_ref[...])
    o_ref[...] = jnp.dot(s, w_ref[...], precision=HIGHEST, preferred_element_type=F32) + b_ref[...]


def _modulation(cvecs, w_mod, b_mod):
    rows, ncols = cvecs.shape[0], w_mod.shape[1]
    tn = 512
    return pl.pallas_call(
        _mod_kernel,
        grid=(ncols // tn,),
        in_specs=[pl.BlockSpec((rows, D_MODEL), lambda j: (0, 0)),
                  pl.BlockSpec((D_MODEL, tn), lambda j: (0, j)),
                  pl.BlockSpec((1, tn), lambda j: (0, j))],
        out_specs=pl.BlockSpec((rows, tn), lambda j: (0, j)),
        out_shape=jax.ShapeDtypeStruct((rows, ncols), F32),
        name="modulation",
    )(cvecs, w_mod, b_mod.reshape(1, ncols))


def _inproj_kernel(x_ref, shift_ref, scale_ref, w_ref, cw_ref, cb_ref, wdt_ref, dtb_ref,
                   p_ref, dt_ref, h_scr, c_scr, *, segs, row_len, tb):
    sub = PROJ_SUB
    npiece = sub // row_len
    pitch = row_len + CONV_GAP
    zgap = jnp.zeros((CONV_GAP, PROJ_TN), F32)
    for r in range(npiece + 1):
        c_scr[r * pitch:r * pitch + CONV_GAP, :] = zgap

    def sub_block(s, carry):
        rows = pl.ds(pl.multiple_of(s * sub, sub), sub)
        h = _layer_norm(x_ref[0, rows, :]) * (1.0 + scale_ref[0]) + shift_ref[0]
        hb = h.astype(BF16)
        h_scr[...] = hb
        dt2 = jnp.dot(hb, wdt_ref[...], preferred_element_type=F32)
        dt_ref[0, rows, :] = _softplus(dt2[:, :DT_W] + dt2[:, DT_W:] + dtb_ref[...])
        for lo, hi, kind in segs:
            for j in range(lo, hi):
                cols = slice(j * PROJ_TN, (j + 1) * PROJ_TN)
                acc = jnp.dot(h_scr[...], w_ref[:, cols], preferred_element_type=F32)
                cb = cb_ref[:, cols]
                if kind in ("conv_silu", "conv"):
                    cw = cw_ref[:, cols]
                    for r in range(npiece):
                        base = CONV_GAP + r * pitch
                        c_scr[base:base + row_len, :] = acc[r * row_len:(r + 1) * row_len, :]
                    pieces = []
                    for r in range(npiece):
                        base = CONV_GAP + r * pitch
                        o = cb + acc[r * row_len:(r + 1) * row_len, :] * cw[2:3]
                        o = o + c_scr[base - 2:base - 2 + row_len, :] * cw[0:1]
                        o = o + c_scr[base - 1:base - 1 + row_len, :] * cw[1:2]
                        o = o + c_scr[base + 1:base + 1 + row_len, :] * cw[3:4]
                        pieces.append(o)
                    out = pieces[0] if npiece == 1 else jnp.concatenate(pieces, axis=0)
                    if kind == "conv_silu":
                        out = _silu(out.astype(BF16))
                elif kind == "silu":
                    out = _silu(acc.astype(BF16))
                elif kind == "gelu":
                    ab = acc.astype(BF16)
                    out = 0.5 * ab * (1.0 + jnp.tanh(0.7978845608028654 * (ab + 0.044715 * ab * ab * ab)))
                elif kind == "sigmoid_bias":
                    out = _sigmoid((acc + cb).astype(BF16))
                else:
                    raise ValueError(kind)
                p_ref[0, rows, cols] = out.astype(BF16)
        return carry

    lax.fori_loop(0, tb // sub, sub_block, 0)


def _inproj(xs, shift, scale, w_all, cw_all, cb_all, w_dt2, dt_bias, *, segs, row_len, tb):
    bsz, t, _ = xs.shape
    ncols = w_all.shape[1]
    kern = functools.partial(_inproj_kernel, segs=segs, row_len=row_len, tb=tb)
    const = lambda shape: pl.BlockSpec(shape, lambda b, i: (0, 0), pipeline_mode=pl.Buffered(1))
    conv_rows = CONV_GAP + (PROJ_SUB // row_len) * (row_len + CONV_GAP)
    return pl.pallas_call(
        kern,
        grid=(bsz, t // tb),
        in_specs=[pl.BlockSpec((1, tb, D_MODEL), lambda b, i: (b, i, 0)),
                  pl.BlockSpec((1, 1, D_MODEL), lambda b, i: (b, 0, 0)),
                  pl.BlockSpec((1, 1, D_MODEL), lambda b, i: (b, 0, 0)),
                  const((D_MODEL, ncols)),
                  const((SUBLANES, ncols)),
                  const((1, ncols)),
                  const((D_MODEL, 2 * DT_W)),
                  const((1, DT_W))],
        out_specs=[pl.BlockSpec((1, tb, ncols), lambda b, i: (b, i, 0)),
                   pl.BlockSpec((1, tb, DT_W), lambda b, i: (b, i, 0))],
        out_shape=[jax.ShapeDtypeStruct((bsz, t, ncols), BF16),
                   jax.ShapeDtypeStruct((bsz, t, DT_W), F32)],
        scratch_shapes=[pltpu.VMEM((PROJ_SUB, D_MODEL), BF16),
                        pltpu.VMEM((conv_rows, PROJ_TN), F32)],
        compiler_params=pltpu.CompilerParams(
            dimension_semantics=("parallel", "parallel"),
            vmem_limit_bytes=VMEM_LIMIT),
        name="inproj",
    )(xs, shift, scale, w_all, cw_all, cb_all, w_dt2, dt_bias)


def _ssd_kernel(*refs, direction, mode, nsteps, nsub, nbatch):
    if mode == "full":
        (xbc_ref, dt_ref, alog_ref, h0_ref, hexp_ref, u_ref, lw_ref, lgb_ref, lam_ref, lh0_ref,
         dskip_ref, yin_ref, hin_ref, g_ref,
         y_ref, hout_ref, s_scr, e_scr, lc_scr, y_scr, cb_scr) = refs
    elif mode == "inter":
        (xbc_ref, dt_ref, alog_ref, h0_ref, hexp_ref, u_ref, lw_ref, lgb_ref, lam_ref, lh0_ref,
         y_ref, hout_ref, s_scr, e_scr, lc_scr) = refs
    else:
        (xbc_ref, dt_ref, alog_ref, hexp_ref, u_ref, lw_ref, lgb_ref, lam_ref,
         sfin_ref, lfin_ref, s_scr, e_scr, lc_scr) = refs
    step = pl.program_id(1)
    L = SSD_CHUNK
    fwd = direction == "fwd"
    groups = [slice(g * SSD_GROUP_W, (g + 1) * SSD_GROUP_W) for g in range(SSD_GROUPS)]
    bcols = [slice(SSD_D_INNER + g * SSD_STATE, SSD_D_INNER + (g + 1) * SSD_STATE) for g in range(SSD_GROUPS)]
    ccols = [slice(SSD_XB + g * SSD_STATE, SSD_XB + (g + 1) * SSD_STATE) for g in range(SSD_GROUPS)]

    @pl.when(step == 0)
    def _():
        if mode == "state":
            s_scr[...] = jnp.zeros(s_scr.shape, F32)
            lc_scr[...] = jnp.zeros(lc_scr.shape, F32)
        else:
            s_scr[...] = h0_ref[...]
            for bb in range(nbatch):
                lc_scr[bb] = jnp.broadcast_to(lh0_ref[bb], (SUBLANES, LRU_WIDTH))

    ldir = 0 if fwd else 1
    chalf = _lru_half_log_a(lam_ref[ldir:ldir + 1, :])
    ldcols = slice(ldir * 2 * LRU_BLOCK_W, (ldir + 1) * 2 * LRU_BLOCK_W)

    def lru_tiles(bb, rows, ks):
        for k in ks:
            cols = slice(k * LRU_BLOCK_W, (k + 1) * LRU_BLOCK_W)
            hv, c = _lru_block(u_ref[bb, rows, cols], lw_ref[k, :, ldcols], lgb_ref[k:k + 1, ldcols],
                               chalf[:, cols], lc_scr[bb, :, cols], fwd)
            lc_scr[bb, :, cols] = c
            if mode == "state":
                continue
            if mode == "full":
                hv = (hv + hin_ref[bb, rows, cols].astype(F32)) * g_ref[bb, rows, cols].astype(F32)
            hout_ref[bb, rows, cols] = hv.astype(hout_ref.dtype)

    a_neg = -jnp.exp(alog_ref[...]) * LOG2E
    ri = lax.broadcasted_iota(jnp.int32, (L, L), 0)
    ci = lax.broadcasted_iota(jnp.int32, (L, L), 1)
    tril = jnp.where(ri >= ci, 1.0, 0.0).astype(BF16)
    triu = jnp.where(ri <= ci, 1.0, 0.0).astype(BF16)

    def expand(vals, col0):
        pieces = _split3_bf16(vals[:, col0:col0 + SSD_HEADS])
        return jnp.dot(jnp.concatenate(pieces, axis=1), hexp_ref[...], preferred_element_type=F32)

    def chunk(i, bb):
        cidx = i if fwd else nsub - 1 - i
        rows = pl.ds(cidx * L, L)
        es = e_scr.at[bb, i]
        ss = s_scr.at[bb]
        if mode == "full":
            ys = y_scr.at[bb, i]
            cs = cb_scr.at[bb, i]
        dt = dt_ref[bb, rows, :]
        dta = dt * a_neg
        cum = _tri_rows(tril if fwd else triu, dta)
        total = cum[L - 1:L, :] if fwd else cum[0:1, :]
        w_state = dt * jnp.exp2(total - cum)
        dec_tot = jnp.broadcast_to(jnp.exp2(total), (SUBLANES, DT_W))
        if mode == "state":
            stack = jnp.concatenate([w_state, dec_tot], axis=0)
        elif mode == "inter":
            stack = jnp.concatenate([w_state, dec_tot, jnp.exp2(cum)], axis=0)
        else:
            stack = jnp.concatenate([w_state, dec_tot, cum], axis=0)
        lru_tiles(bb, rows, (0, 1))
        yield
        es[...] = expand(stack, 0 if fwd else SSD_HEADS)
        r_dec = L
        r_ein = L + SUBLANES
        lru_tiles(bb, rows, (2, 3))
        if mode == "full":
            P = SSD_HEADDIM
            half = L // 2
            cum_b = _tri_rows(triu, dta)
            cs[...] = expand(cum_b, SSD_HEADS)
            dta_t = dta.T
            ldt_t = jnp.log2(dt.T)
            row_f = ldt_t - _tri_cols(dta_t, triu)
            row_b = ldt_t - _tri_cols(dta_t, tril)
            ti = lax.broadcasted_iota(jnp.int32, (half, 2 * P), 0)
            tj = lax.broadcasted_iota(jnp.int32, (half, 2 * P), 1) % P
            lo_tri = ti >= tj
            up_tri = ti <= tj
            lane_a = lax.broadcasted_iota(jnp.int32, (half, 2 * P), 1) < P

            def intra_group(g):
                bg = xbc_ref[bb, rows, bcols[g]]
                cg = xbc_ref[bb, rows, ccols[g]]
                b4 = jnp.concatenate([bg[:half, :], bg[:half, :], bg[half:, :], bg[half:, :]], axis=0)
                cb4 = lax.dot_general(cg, b4, (((1,), (1,)), ((), ())), preferred_element_type=F32)
                for pp in range(SSD_HPG // 2):
                    ha = g * SSD_HPG + 2 * pp
                    hb = SSD_HEADS + ha
                    px = slice(ha * P, (ha + 2) * P)
                    col_f = es[r_ein:r_ein + L, px]
                    col_b = cs[:, px]
                    xp = xbc_ref[bb, rows, px]
                    ms, rhss = [], []
                    for jh in range(2):
                        js = slice(jh * half, (jh + 1) * half)
                        rf = jnp.concatenate([row_f[ha:ha + 1, js], row_f[ha + 1:ha + 2, js]], axis=1)
                        rb = jnp.concatenate([row_b[hb:hb + 1, js], row_b[hb + 1:hb + 2, js]], axis=1)
                        dg = slice(jh * half, (jh + 1) * half)
                        zb = jnp.zeros((half, 2 * P), BF16)
                        m_diag = (jnp.where(lo_tri, jnp.exp2(col_f[dg, :] + rf).astype(BF16), zb)
                                  + jnp.where(up_tri, jnp.exp2(col_b[dg, :] + rb).astype(BF16), zb))
                        if jh == 0:
                            m = jnp.concatenate([m_diag, jnp.exp2(col_f[half:, :] + rf).astype(BF16)], axis=0)
                        else:
                            m = jnp.concatenate([jnp.exp2(col_b[:half, :] + rb).astype(BF16), m_diag], axis=0)
                        ms.append(m * cb4[:, jh * 2 * P:(jh + 1) * 2 * P].astype(BF16))
                        xj = xp[js, :]
                        zero = jnp.zeros_like(xj)
                        rhss += [jnp.where(lane_a, xj, zero), jnp.where(lane_a, zero, xj)]
                    y2 = jnp.dot(jnp.concatenate(ms, axis=1), jnp.concatenate(rhss, axis=0),
                                 preferred_element_type=F32)
                    y_ref[bb, rows, px] = (ys[:, px] + y2).astype(y_ref.dtype)

        yield

        if mode != "state":
            for g, gx in enumerate(groups):
                cg = xbc_ref[bb, rows, ccols[g]]
                e_in = es[r_ein:r_ein + L, gx]
                if mode == "full":
                    e_in = jnp.exp2(e_in)
                y_g = e_in * jnp.dot(cg, ss[:, gx].astype(BF16), preferred_element_type=F32)
                if mode == "inter":
                    y_ref[bb, rows, gx] = y_g.astype(y_ref.dtype)
                else:
                    xg = xbc_ref[bb, rows, gx].astype(F32)
                    ys[:, gx] = y_g + yin_ref[bb, rows, gx].astype(F32) + dskip_ref[:, gx] * xg

        lru_tiles(bb, rows, (4, 5))
        yield
        if mode == "full":
            for g in range(SSD_GROUPS // 2):
                intra_group(g)
                yield
        for g, gx in enumerate(groups):
            bg = xbc_ref[bb, rows, bcols[g]]
            xw = xbc_ref[bb, rows, gx] * es[0:L, gx].astype(BF16)
            upd = lax.dot_general(bg, xw, (((0,), (0,)), ((), ())), preferred_element_type=F32)
            ss[:, gx] = es[r_dec:r_dec + 1, gx] * ss[:, gx] + upd
        lru_tiles(bb, rows, (6, 7))
        yield

        if mode == "full":
            for g in range(SSD_GROUPS // 2, SSD_GROUPS):
                intra_group(g)
                yield

    gens = [[chunk(i, bb) for bb in range(nbatch)] for i in range(nsub)]
    for _ in range(2):
        for i in range(nsub):
            for gen in gens[i]:
                next(gen)
    for i in range(nsub):
        for _ in zip(*gens[i]):
            pass

    if mode == "state":
        @pl.when(step == nsteps - 1)
        def _():
            sfin_ref[...] = s_scr[...]
            for bb in range(nbatch):
                lfin_ref[bb] = lc_scr[bb, 0:1, :]


def _ssd_sweep(xbc, dt, alog_row, h0, hexp, *, direction, mode, xbc_width, lru=None, dskip=None, yin=None,
               hin=None):
    bsz, t, _ = xbc.shape
    tb = min(t, SSD_TB)
    nb = SSD_NB if bsz % SSD_NB == 0 else 1
    nsteps = t // tb
    hp = SSD_D_INNER
    w = LRU_WIDTH
    row = (lambda c: c) if direction == "fwd" else (lambda c: nsteps - 1 - c)
    cmap = lambda b, c: (b, row(c), 0)
    const = lambda shape: pl.BlockSpec(shape, lambda b, c: (0,) * len(shape))
    in_specs = [pl.BlockSpec((nb, tb, xbc_width), cmap),
                pl.BlockSpec((nb, tb, DT_W), cmap),
                const((1, DT_W))]
    args = [xbc, dt, alog_row]
    if mode != "state":
        in_specs.append(pl.BlockSpec((nb, SSD_STATE, hp), lambda b, c: (b, 0, 0)))
        args.append(h0)
    in_specs.append(const((3 * SSD_HEADS, hp)))
    args.append(jnp.concatenate([hexp, hexp, hexp], axis=0))
    stack_rows = (SSD_CHUNK + SUBLANES) if mode == "state" else (2 * SSD_CHUNK + SUBLANES)
    nsub = tb // SSD_CHUNK
    scratch = [pltpu.VMEM((nb, SSD_STATE, hp), F32), pltpu.VMEM((nb, nsub, stack_rows, hp), F32)]
    proj, u_col, g_col, w_blk, gb_blk, lam, lh0 = lru
    in_specs += [pl.BlockSpec((nb, tb, w), lambda b, c: (b, row(c), u_col)),
                 const((LRU_BLOCKS, LRU_BLOCK_W, 4 * LRU_BLOCK_W)),
                 const((LRU_BLOCKS, 4 * LRU_BLOCK_W)),
                 const((2, w))]
    args += [proj, w_blk, gb_blk, lam]
    if mode != "state":
        in_specs.append(pl.BlockSpec((nb, 1, w), lambda b, c: (b, 0, 0)))
        args.append(lh0)
    scratch.append(pltpu.VMEM((nb, SUBLANES, w), F32))
    if mode == "full":
        scratch += [pltpu.VMEM((nb, nsub, SSD_CHUNK, hp), F32), pltpu.VMEM((nb, nsub, SSD_CHUNK, hp), F32)]
        in_specs += [const((1, hp)), pl.BlockSpec((nb, tb, hp), cmap), pl.BlockSpec((nb, tb, w), cmap),
                     pl.BlockSpec((nb, tb, w), lambda b, c: (b, row(c), g_col))]
        args += [dskip, yin, hin, proj]
    if mode == "state":
        out_specs = [pl.BlockSpec((nb, SSD_STATE, hp), lambda b, c: (b, 0, 0)),
                     pl.BlockSpec((nb, 1, w), lambda b, c: (b, 0, 0))]
        out_shape = [jax.ShapeDtypeStruct((bsz, SSD_STATE, hp), F32), jax.ShapeDtypeStruct((bsz, 1, w), F32)]
    else:
        out_specs = [pl.BlockSpec((nb, tb, hp), cmap), pl.BlockSpec((nb, tb, w), cmap)]
        out_shape = [jax.ShapeDtypeStruct((bsz, t, hp), BF16), jax.ShapeDtypeStruct((bsz, t, w), BF16)]
    kern = functools.partial(_ssd_kernel, direction=direction, mode=mode, nsteps=nsteps, nsub=nsub, nbatch=nb)
    return pl.pallas_call(
        kern,
        grid=(bsz // nb, nsteps),
        in_specs=in_specs,
        out_specs=out_specs,
        out_shape=out_shape,
        scratch_shapes=scratch,
        compiler_params=pltpu.CompilerParams(
            dimension_semantics=("parallel", "arbitrary"),
            vmem_limit_bytes=VMEM_LIMIT),
        name=f"ssd_{direction}_{mode}",
    )(*args)


def _lru_half_log_a(lam_row):
    return (-0.5 * LRU_C) * _softplus(-lam_row)


def _lru_block(ub, wk, gbk, chalf_k, c, fwd):
    bw = LRU_BLOCK_W
    pre = jnp.dot(ub, wk, preferred_element_type=F32) + gbk
    nsl = LRU_TC // LRU_VT
    hs = [None] * nsl
    for si in (range(nsl) if fwd else range(nsl - 1, -1, -1)):
        rs = slice(si * LRU_VT, (si + 1) * LRU_VT)
        tr = jnp.tanh(pre[rs, :bw])
        ti = jnp.tanh(pre[rs, bw:])
        log_a = tr * chalf_k + chalf_k
        a = jnp.exp(log_a)
        th = jnp.tanh(log_a)
        b = jnp.sqrt(-0.5 * th / (1.0 - th)) * ((ti + 1.0) * ub[rs, :].astype(F32))
        ngrp = LRU_VT // SUBLANES
        a_t = pltpu.einshape("gsl->sgl", a.reshape(ngrp, SUBLANES, bw))
        b_t = pltpu.einshape("gsl->sgl", b.reshape(ngrp, SUBLANES, bw))
        h_loc, p_loc = [None] * SUBLANES, [None] * SUBLANES
        h = p = None
        for s in (range(SUBLANES) if fwd else range(SUBLANES - 1, -1, -1)):
            h = b_t[s] if h is None else a_t[s] * h + b_t[s]
            p = a_t[s] if p is None else a_t[s] * p
            h_loc[s], p_loc[s] = h, p
        cur = c[0:1, :]
        h_in = [None] * ngrp
        for gi in (range(ngrp) if fwd else range(ngrp - 1, -1, -1)):
            h_in[gi] = cur
            cur = p[gi:gi + 1, :] * cur + h[gi:gi + 1, :]
        c = jnp.broadcast_to(cur, (SUBLANES, bw))
        h_in = jnp.concatenate(h_in, axis=0)
        h_t = jnp.stack([h_loc[s] + p_loc[s] * h_in for s in range(SUBLANES)], axis=0)
        hs[si] = pltpu.einshape("sgl->gsl", h_t).reshape(LRU_VT, bw)
    return jnp.concatenate(hs, axis=0), c


def _post_kernel(x_ref, y_ref, z_ref, gate_ref, ylru_ref, mod_ref, nw_ref, wbs_ref, wbl_ref, wout_ref,
                 ln1g_ref, ln1b_ref, w1_ref, b1_ref, w2_ref, b2_ref, ln2g_ref, ln2b_ref, o_ref, *, tb):
    d = D_MODEL
    mod = mod_ref[0]
    gate1 = mod[:, 2 * d:3 * d]
    shift2 = mod[:, 3 * d:4 * d]
    scale2 = mod[:, 4 * d:5 * d]
    gate2 = mod[:, 5 * d:6 * d]

    def slab(rows):
        u = y_ref[0, rows, :].astype(F32) * z_ref[0, rows, :].astype(F32)
        parts = []
        for g in range(SSD_GROUPS):
            ug = u[:, g * SSD_GROUP_W:(g + 1) * SSD_GROUP_W]
            ms = jnp.mean(ug * ug, axis=-1, keepdims=True)
            parts.append(ug * lax.rsqrt(ms + RMS_EPS))
        un = (jnp.concatenate(parts, axis=1) * nw_ref[...]).astype(BF16)
        yield
        br_ssd = jnp.dot(un, wbs_ref[...], preferred_element_type=F32)
        br_lru = jnp.dot(ylru_ref[0, rows, :], wbl_ref[...], preferred_element_type=F32)
        gates = gate_ref[0, rows, :].astype(F32)
        merged = (gates[:, :d] * br_ssd + gates[:, d:] * br_lru).astype(BF16)
        yield
        x_mix = jnp.dot(merged, wout_ref[...], preferred_element_type=F32)
        x1 = _layer_norm(DEEPNORM_ALPHA * x_ref[0, rows, :] + gate1 * x_mix) * ln1g_ref[...] + ln1b_ref[...]
        h2 = (_layer_norm(x1) * (1.0 + scale2) + shift2).astype(BF16)
        yield
        hid = jnp.dot(h2, w1_ref[...], preferred_element_type=F32) + b1_ref[...]
        hid = jnp.maximum(hid, 0.0)
        hid = (hid * hid).astype(BF16)
        yield
        mlp = jnp.dot(hid, w2_ref[...], preferred_element_type=F32) + b2_ref[...]
        o_ref[0, rows, :] = _layer_norm(DEEPNORM_ALPHA * x1 + gate2 * mlp) * ln2g_ref[...] + ln2b_ref[...]
        yield

    gens = [slab(pl.ds(r * POST_SUB, POST_SUB)) for r in range(tb // POST_SUB)]
    nstage = 5
    for tick in range(nstage + len(gens) - 1):
        for r, gen in enumerate(gens):
            if 0 <= tick - r < nstage:
                next(gen)


def _const_spec(shape):
    nd = len(shape)
    return pl.BlockSpec(shape, lambda b, i: (0,) * nd, pipeline_mode=pl.Buffered(1))


def _post(x, y_ssd, proj, y_lru, mod, norm_w, w_br_ssd, w_br_lru, w_out, ln1_g, ln1_b,
          w1, b1, w2, b2, ln2_g, ln2_b, *, tb, z_col, gate_col):
    bsz, t, d = x.shape
    di = SSD_D_INNER
    row = lambda v: v.reshape(1, -1)
    return pl.pallas_call(
        functools.partial(_post_kernel, tb=tb),
        grid=(bsz, t // tb),
        in_specs=[pl.BlockSpec((1, tb, d), lambda b, i: (b, i, 0)),
                  pl.BlockSpec((1, tb, di), lambda b, i: (b, i, 0)),
                  pl.BlockSpec((1, tb, di), lambda b, i: (b, i, z_col)),
                  pl.BlockSpec((1, tb, di), lambda b, i: (b, i, gate_col)),
                  pl.BlockSpec((1, tb, d), lambda b, i: (b, i, 0)),
                  pl.BlockSpec((1, 1, N_MOD * d), lambda b, i: (b, 0, 0)),
                  _const_spec((1, di)),
                  _const_spec((di, d)),
                  _const_spec((d, d)),
                  _const_spec((d, d)),
                  _const_spec((1, d)),
                  _const_spec((1, d)),
                  _const_spec((d, MLP_HIDDEN)),
                  _const_spec((1, MLP_HIDDEN)),
                  _const_spec((MLP_HIDDEN, d)),
                  _const_spec((1, d)),
                  _const_spec((1, d)),
                  _const_spec((1, d))],
        out_specs=pl.BlockSpec((1, tb, d), lambda b, i: (b, i, 0)),
        out_shape=jax.ShapeDtypeStruct((bsz, t, d), F32),
        compiler_params=pltpu.CompilerParams(
            dimension_semantics=("parallel", "parallel"),
            vmem_limit_bytes=VMEM_LIMIT),
        name="post",
    )(x, y_ssd, proj, proj, y_lru, mod, row(norm_w), w_br_ssd.astype(BF16), w_br_lru.astype(BF16),
      w_out.astype(BF16), row(ln1_g), row(ln1_b), w1.astype(BF16), row(b1), w2.astype(BF16), row(b2),
      row(ln2_g), row(ln2_b))


def _pad_rows(v, n):
    return jnp.pad(v, ((0, n - v.shape[0]), (0, 0)))


def kernel(x, c, ctx, c_ctx, w_mod, b_mod, w_in, b_gate, ssd_conv_w, ssd_conv_b, ssd_dt_bias, ssd_a_log,
           ssd_d, ssd_norm_w, lru_conv_w, lru_conv_b, lru_wa, lru_ba, lru_wi, lru_bi, lru_lambda, w_br_ssd,
           w_br_lru, w_out, ln1_g, ln1_b, w_mlp1, b_mlp1, w_mlp2, b_mlp2, ln2_g, ln2_b):
    bsz, t, d = x.shape
    tctx = ctx.shape[1]
    l = 0
    wi = w_in[l]

    nrow = -(-(bsz + 1) // SUBLANES) * SUBLANES
    cvecs = _pad_rows(jnp.concatenate([c, c_ctx[None, :]], axis=0), nrow)
    mods = _modulation(cvecs, w_mod[l], b_mod[l])
    mod_x = mods[:bsz].reshape(bsz, 1, N_MOD * d)
    shift_x, scale_x = mod_x[:, :, :d], mod_x[:, :, d:2 * d]
    shift_c = jnp.broadcast_to(mods[bsz:bsz + 1, :d].reshape(1, 1, d), (bsz, 1, d))
    scale_c = jnp.broadcast_to(mods[bsz:bsz + 1, d:2 * d].reshape(1, 1, d), (bsz, 1, d))

    zc = lambda n: jnp.zeros((CONV_K, n), F32)
    z1 = lambda n: jnp.zeros((n,), F32)
    w_x = jnp.concatenate([wi[:, :SSD_XB], wi[:, O_C:O_Z], wi[:, O_Z:O_LRU_GATE], wi[:, O_MERGE:],
                           wi[:, O_LRU:STATE_COLS], wi[:, O_LRU_GATE:O_MERGE]], axis=1).astype(BF16)
    cw_x = _pad_rows(jnp.concatenate([ssd_conv_w[l], zc(SSD_D_INNER), zc(2 * d), lru_conv_w[l], zc(LRU_WIDTH)],
                                     axis=1), SUBLANES)
    cb_x = jnp.concatenate([ssd_conv_b[l], z1(SSD_D_INNER), b_gate[l], lru_conv_b[l], z1(LRU_WIDTH)])[None, :]
    n_xbc, n_z, n_m, n_l = SSD_XBC // PROJ_TN, SSD_D_INNER // PROJ_TN, 2 * d // PROJ_TN, LRU_WIDTH // PROJ_TN
    segs_x, lo = [], 0
    for n, kind in ((n_xbc, "conv_silu"), (n_z, "silu"), (n_m, "sigmoid_bias"), (n_l, "conv"), (n_l, "gelu")):
        segs_x.append((lo, lo + n, kind))
        lo += n
    z_col = SSD_XBC // SSD_D_INNER
    gate_col = z_col + 1
    lru_col = (SSD_XBC + 2 * SSD_D_INNER) // LRU_WIDTH
    lrug_col = lru_col + 1

    w_c = jnp.concatenate([wi[:, :SSD_XB], wi[:, O_LRU:STATE_COLS]], axis=1).astype(BF16)
    cw_c = _pad_rows(jnp.concatenate([ssd_conv_w[l][:, :SSD_XB], lru_conv_w[l]], axis=1), SUBLANES)
    cb_c = jnp.concatenate([ssd_conv_b[l][:SSD_XB], lru_conv_b[l]])[None, :]
    segs_c = [(0, SSD_XB // PROJ_TN, "conv_silu"), (SSD_XB // PROJ_TN, SSD_XB // PROJ_TN + n_l, "conv")]
    lru_col_c = SSD_XB // LRU_WIDTH

    w_dt = jnp.pad(wi[:, O_DT:O_LRU], ((0, 0), (0, DT_W - SSD_DT)))
    w_dt_hi = w_dt.astype(BF16)
    w_dt_lo = (w_dt - w_dt_hi.astype(F32)).astype(BF16)
    w_dt2 = jnp.concatenate([w_dt_hi, w_dt_lo], axis=1)
    dt_bias = jnp.pad(ssd_dt_bias[l].reshape(1, SSD_DT), ((0, 0), (0, DT_W - SSD_DT)))
    alog_row = jnp.pad(ssd_a_log[l].reshape(1, SSD_DT), ((0, 0), (0, DT_W - SSD_DT)))
    dskip = jnp.repeat(ssd_d[l], SSD_HEADDIM)[None, :]
    hexp = jnp.repeat(jnp.eye(SSD_HEADS, dtype=BF16), SSD_HEADDIM, axis=1)

    w_blk = (0.5 * jnp.concatenate([lru_wa[l, 0], lru_wi[l, 0], lru_wa[l, 1], lru_wi[l, 1]], axis=-1)).astype(BF16)
    blk = lambda v: v.reshape(LRU_BLOCKS, LRU_BLOCK_W)
    gb_blk = 0.5 * jnp.concatenate([blk(lru_ba[l, 0]), blk(lru_bi[l, 0]), blk(lru_ba[l, 1]), blk(lru_bi[l, 1])],
                                   axis=-1)
    lam = lru_lambda[l]

    proj_c, dt_c = _inproj(ctx, shift_c, scale_c, w_c, cw_c, cb_c, w_dt2, dt_bias,
                           segs=segs_c, row_len=tctx, tb=tctx)
    lru_c = (proj_c, lru_col_c, None, w_blk, gb_blk, lam, None)
    s_f, l_f = _ssd_sweep(proj_c, dt_c, alog_row, None, hexp, direction="fwd", mode="state", xbc_width=SSD_XB,
                          lru=lru_c)
    s_b, l_b = _ssd_sweep(proj_c, dt_c, alog_row, None, hexp, direction="bwd", mode="state", xbc_width=SSD_XB,
                          lru=lru_c)

    proj_x, dt_x = _inproj(x, shift_x, scale_x, w_x, cw_x, cb_x, w_dt2, dt_bias,
                           segs=segs_x, row_len=GRID_W, tb=min(t, PROJ_TB))
    y_b, h_b = _ssd_sweep(proj_x, dt_x, alog_row, s_b, hexp, direction="bwd", mode="inter", xbc_width=SSD_XBC,
                          lru=(proj_x, lru_col, lrug_col, w_blk, gb_blk, lam, l_b))
    y_ssd, y_lru = _ssd_sweep(proj_x, dt_x, alog_row, s_f, hexp, direction="fwd", mode="full", xbc_width=SSD_XBC,
                              lru=(proj_x, lru_col, lrug_col, w_blk, gb_blk, lam, l_f),
                              dskip=dskip, yin=y_b, hin=h_b)
    return _post(x, y_ssd, proj_x, y_lru, mod_x, ssd_norm_w[l], w_br_ssd[l], w_br_lru[l], w_out[l],
                 ln1_g[l], ln1_b[l], w_mlp1[l], b_mlp1[l], w_mlp2[l], b_mlp2[l], ln2_g[l], ln2_b[l],
                 tb=min(t, POST_TB), z_col=z_col, gate_col=gate_col)
```
